```python
import math
import jax
import jax.numpy as jnp
from jax import lax
import numpy as np

D_MODEL = 1024
BATCH = 16
SEQ = 4096
DEPTH = 2

D_MIX = D_MODEL
D_FF = 4 * D_MODEL
HEAD_DIM = 64
C_R = D_MIX // 4
H_R = C_R // HEAD_DIM
R_DECAY = 32
R_AAA = 32
R_GATE = 64
RWKV_GN_EPS = 64e-5
C_M = D_MIX // 2
H_M = C_M // HEAD_DIM
SSM_GROUPS = 2
SSM_STATE = 128
CONV_K = 4
CONV_DIM = C_M + 2 * SSM_GROUPS * SSM_STATE
CHUNK = 128
SSM_NORM_EPS = 1e-5
C_F = D_MIX - C_R - C_M
H_F = C_F // HEAD_DIM
Q_BLOCK = 128
RWKV_COLS = 3 * C_R + R_DECAY + R_AAA + R_GATE
SSM_COLS = C_M + CONV_DIM + H_M
FOX_COLS = 3 * C_F + H_F
D_IN = RWKV_COLS + SSM_COLS + FOX_COLS
NORM_EPS = 1e-6

kernel_name = 'hybrid_rwkv7_mamba2_fox_trunk'


def rms_norm(x, g, eps=NORM_EPS):
    xf = x.astype(jnp.float32)
    y = xf * lax.rsqrt(jnp.mean(xf * xf, axis=-1, keepdims=True) + eps)
    return (y * g.astype(jnp.float32)).astype(x.dtype)


def token_shift(t):
    return jnp.pad(t, ((0, 0), (1, 0), (0, 0)))[:, :-1]


def causal_depthwise_conv(x, w, b):
    out = lax.conv_general_dilated(
        x, w[:, None, :].astype(x.dtype), window_strides=(1,),
        padding=[(CONV_K - 1, 0)], dimension_numbers=('NWC', 'WIO', 'NWC'),
        feature_group_count=x.shape[-1])
    return out + b


def rwkv7_recurrence(r, w, k, v, kk, b):
    def step(S, inp):
        r_t, w_t, k_t, v_t, kk_t, b_t = inp
        sa = jnp.einsum('bhij,bhj->bhi', S, kk_t)
        S = (S * w_t[:, :, None, :] - sa[..., None] * b_t[:, :, None, :]
             + v_t[..., None] * k_t[:, :, None, :])
        return S, jnp.einsum('bhij,bhj->bhi', S, r_t)
    Bsz, _, H, N = r.shape
    s0 = jnp.zeros((Bsz, H, N, N), jnp.float32)
    xs = tuple(jnp.moveaxis(t, 1, 0) for t in (r, w, k, v, kk, b))
    _, y = lax.scan(step, s0, xs)
    return jnp.moveaxis(y, 0, 1)


def rwkv7_group(p, mu, w0, w2, a0, a2, g2, k_k, k_a, r_k, ln_w, ln_b):
    Bsz, T, _ = p.shape
    p = p + (token_shift(p) - p) * mu
    r, k, v, w_lo, a_lo, g_lo = jnp.split(
        p, [C_R, 2 * C_R, 3 * C_R, 3 * C_R + R_DECAY, 3 * C_R + R_DECAY + R_AAA], axis=-1)
    logw = (w0 + jnp.tanh(w_lo) @ w2).astype(jnp.float32)
    decay = jnp.exp(-jnp.exp(-jax.nn.softplus(-logw) - 0.5))
    a = jax.nn.sigmoid(a0 + a_lo @ a2)
    g = jax.nn.sigmoid(g_lo) @ g2
    kk = k * k_k
    k = k * (1.0 + (a - 1.0) * k_a)

    def heads(t):
        return t.reshape(Bsz, T, H_R, HEAD_DIM).astype(jnp.float32)

    kk = heads(kk)
    kk = kk / jnp.maximum(jnp.sqrt(jnp.sum(kk * kk, axis=-1, keepdims=True)), 1e-12)
    rh, kh, vh, ah, dh = heads(r), heads(k), heads(v), heads(a), heads(decay)
    y = rwkv7_recurrence(rh, dh, kh, vh, kk, kk * ah)
    m = jnp.mean(y, axis=-1, keepdims=True)
    var = jnp.mean(jnp.square(y - m), axis=-1, keepdims=True)
    y = (y - m) * lax.rsqrt(var + RWKV_GN_EPS)
    y = y.reshape(Bsz, T, C_R) * ln_w.astype(jnp.float32) + ln_b.astype(jnp.float32)
    bonus = jnp.sum(rh * kh * r_k.astype(jnp.float32), axis=-1, keepdims=True) * vh
    y = (y + bonus.reshape(Bsz, T, C_R)) * g.astype(jnp.float32)
    return y.astype(p.dtype)


def ssd_chunked(xh, dt, A, Bm, Cm):
    Bsz, T, H, P = xh.shape
    G, N = Bm.shape[2], Bm.shape[3]
    E = H // G
    nc = T // CHUNK
    x = (xh * dt[..., None]).reshape(Bsz, nc, CHUNK, G, E, P)
    a = (dt * A).reshape(Bsz, nc, CHUNK, G, E)
    Bc = Bm.reshape(Bsz, nc, CHUNK, G, N)
    Cc = Cm.reshape(Bsz, nc, CHUNK, G, N)
    a_cum = jnp.cumsum(a, axis=2)
    seg = a_cum[:, :, :, None] - a_cum[:, :, None, :]
    causal = (jnp.arange(CHUNK)[:, None] >= jnp.arange(CHUNK)[None, :])[None, None, :, :, None, None]
    decay = jnp.exp(jnp.where(causal, seg, -jnp.inf))
    cb = jnp.einsum('bclgn,bcsgn->bclsg', Cc, Bc)
    y_diag = jnp.einsum('bclsge,bcsgep->bclgep', cb[..., None] * decay, x)
    decay_to_end = jnp.exp(a_cum[:, :, -1:] - a_cum)
    states = jnp.einsum('bclgn,bclge,bclgep->bcgepn', Bc, decay_to_end, x)
    chunk_decay = jnp.exp(a_cum[:, :, -1])

    def chunk_step(h, inp):
        st, dec = inp
        return h * dec[..., None, None] + st, h

    h0 = jnp.zeros((Bsz, G, E, P, N), jnp.float32)
    _, prev = lax.scan(chunk_step, h0, (jnp.moveaxis(states, 1, 0), jnp.moveaxis(chunk_decay, 1, 0)))
    prev = jnp.moveaxis(prev, 0, 1)
    y_off = jnp.einsum('bclgn,bcgepn,bclge->bclgep', Cc, prev, jnp.exp(a_cum))
    return (y_diag + y_off).reshape(Bsz, T, H, P)


def mamba2_group(p, conv_w, conv_b, dt_bias, A_log, D, norm_w):
    Bsz, T, _ = p.shape
    z, xbc, dt = jnp.split(p, [C_M, C_M + CONV_DIM], axis=-1)
    xbc = jax.nn.silu(causal_depthwise_conv(xbc, conv_w, conv_b))
    xs, Bm, Cm = jnp.split(xbc, [C_M, C_M + SSM_GROUPS * SSM_STATE], axis=-1)
    dt = jax.nn.softplus(dt.astype(jnp.float32) + dt_bias.astype(jnp.float32))
    A = -jnp.exp(A_log.astype(jnp.float32))
    xh = xs.reshape(Bsz, T, H_M, HEAD_DIM).astype(jnp.float32)
    y = ssd_chunked(xh, dt, A,
                    Bm.reshape(Bsz, T, SSM_GROUPS, SSM_STATE).astype(jnp.float32),
                    Cm.reshape(Bsz, T, SSM_GROUPS, SSM_STATE).astype(jnp.float32))
    y = y + D.astype(jnp.float32)[:, None] * xh
    y = y.reshape(Bsz, T, C_M) * jax.nn.silu(z.astype(jnp.float32))
    y = y.reshape(Bsz, T, SSM_GROUPS, C_M // SSM_GROUPS)
    y = y * lax.rsqrt(jnp.mean(y * y, axis=-1, keepdims=True) + SSM_NORM_EPS)
    return (y.reshape(Bsz, T, C_M) * norm_w.astype(jnp.float32)).astype(p.dtype)


def forgetting_attention(q, k, v, log_f):
    Bsz, T, H, Dh = q.shape
    nb = T // Q_BLOCK
    scale = 1.0 / math.sqrt(Dh)
    cT = jnp.cumsum(log_f, axis=1).transpose(0, 2, 1)
    qb = jnp.moveaxis(q.reshape(Bsz, nb, Q_BLOCK, H, Dh), 1, 0)
    cb = jnp.moveaxis(cT.reshape(Bsz, H, nb, Q_BLOCK), 2, 0)
    starts = jnp.arange(nb, dtype=jnp.int32) * Q_BLOCK
    key_pos = jnp.arange(T, dtype=jnp.int32)

    def block(args):
        q_i, c_i, s0 = args
        s = jnp.einsum('bqhd,bkhd->bhqk', q_i, k) * scale
        s = s + (c_i[..., :, None] - cT[..., None, :])
        mask = (s0 + jnp.arange(Q_BLOCK, dtype=jnp.int32))[:, None] >= key_pos[None, :]
        s = jnp.where(mask[None, None], s, -jnp.inf)
        pr = jax.nn.softmax(s, axis=-1)
        return jnp.einsum('bhqk,bkhd->bqhd', pr, v)

    out = lax.map(block, (qb, cb, starts))
    return jnp.moveaxis(out, 0, 1).reshape(Bsz, T, H, Dh)


def fox_group(p, f_bias, norm_w):
    Bsz, T, _ = p.shape
    q, k, v, f = jnp.split(p, [C_F, 2 * C_F, 3 * C_F], axis=-1)

    def heads(t):
        return t.reshape(Bsz, T, H_F, HEAD_DIM).astype(jnp.float32)

    log_f = jax.nn.log_sigmoid(f.astype(jnp.float32) + f_bias.astype(jnp.float32))
    y = forgetting_attention(heads(q), heads(k), heads(v), log_f)
    y = y * lax.rsqrt(jnp.mean(y * y, axis=-1, keepdims=True) + NORM_EPS)
    return (y.reshape(Bsz, T, C_F) * norm_w.astype(jnp.float32)).astype(p.dtype)


def setup_inputs(seed: int = 0) -> dict:
    key = jax.random.key(seed)
    ks = jax.random.split(key, 28)
    L = DEPTH

    def nrm(k, shape, scale):
        return jax.random.normal(k, shape, jnp.float32) * scale

    def gain(k, shape):
        return 1.0 + 0.02 * jax.random.normal(k, shape, jnp.float32)

    def unif(k, shape, lo, hi):
        return jax.random.uniform(k, shape, jnp.float32, minval=lo, maxval=hi)

    dt0 = jnp.exp(unif(ks[17], (L, H_M), math.log(1e-3), math.log(1e-1)))
    return {
        'x': nrm(ks[0], (BATCH, SEQ, D_MODEL), 1.0),
        'norm_mix_pre': gain(ks[1], (L, D_MODEL)),
        'norm_mix_post': gain(ks[2], (L, D_MODEL)),
        'w_in': nrm(ks[3], (L, D_MODEL, D_IN), D_MODEL ** -0.5),
        'rwkv_mu': unif(ks[4], (L, RWKV_COLS), 0.0, 1.0),
        'rwkv_w0': unif(ks[5], (L, C_R), -6.0, -1.0),
        'rwkv_w2': nrm(ks[6], (L, R_DECAY, C_R), 0.1 * R_DECAY ** -0.5),
        'rwkv_a0': nrm(ks[7], (L, C_R), 0.1),
        'rwkv_a2': nrm(ks[8], (L, R_AAA, C_R), 0.5 * R_AAA ** -0.5),
        'rwkv_g2': nrm(ks[9], (L, R_GATE, C_R), R_GATE ** -0.5),
        'rwkv_k_k': 0.85 + nrm(ks[10], (L, C_R), 0.05),
        'rwkv_k_a': 1.0 + nrm(ks[11], (L, C_R), 0.05),
        'rwkv_r_k': nrm(ks[12], (L, H_R, HEAD_DIM), 0.1),
        'rwkv_ln_w': gain(ks[13], (L, C_R)),
        'rwkv_ln_b': nrm(ks[14], (L, C_R), 0.02),
        'ssm_conv_w': nrm(ks[15], (L, CONV_K, CONV_DIM), 0.5),
        'ssm_conv_b': nrm(ks[16], (L, CONV_DIM), 0.02),
        'ssm_dt_bias': dt0 + jnp.log(-jnp.expm1(-dt0)),
        'ssm_A_log': jnp.log(unif(ks[18], (L, H_M), 1.0, 16.0)),
        'ssm_D': 1.0 + nrm(ks[19], (L, H_M), 0.1),
        'ssm_norm_w': gain(ks[20], (L, C_M)),
        'fox_f_bias': unif(ks[21], (L, H_F), 1.0, 5.0),
        'fox_norm_w': gain(ks[22], (L, C_F)),
        'w_out': nrm(ks[23], (L, D_MIX, D_MODEL), D_MIX ** -0.5),
        'norm_mlp_pre': gain(ks[24], (L, D_MODEL)),
        'norm_mlp_post': gain(ks[25], (L, D_MODEL)),
        'w_mlp_up': nrm(ks[26], (L, D_MODEL, D_FF), D_MODEL ** -0.5),
        'w_mlp_down': nrm(ks[27], (L, D_FF, D_MODEL), D_FF ** -0.5),
    }


def reference(x, norm_mix_pre, norm_mix_post, w_in, rwkv_mu, rwkv_w0, rwkv_w2, rwkv_a0,
              rwkv_a2, rwkv_g2, rwkv_k_k, rwkv_k_a, rwkv_r_k, rwkv_ln_w, rwkv_ln_b,
              ssm_conv_w, ssm_conv_b, ssm_dt_bias, ssm_A_log, ssm_D, ssm_norm_w,
              fox_f_bias, fox_norm_w, w_out, norm_mlp_pre, norm_mlp_post, w_mlp_up,
              w_mlp_down):
    h = x
    for l in range(DEPTH):
        u = rms_norm(h, norm_mix_pre[l])
        proj = u @ w_in[l]
        p_r, p_m, p_f = jnp.split(proj, [RWKV_COLS, RWKV_COLS + SSM_COLS], axis=-1)
        y_r = rwkv7_group(p_r, rwkv_mu[l], rwkv_w0[l], rwkv_w2[l], rwkv_a0[l], rwkv_a2[l],
                          rwkv_g2[l], rwkv_k_k[l], rwkv_k_a[l], rwkv_r_k[l],
                          rwkv_ln_w[l], rwkv_ln_b[l])
        y_m = mamba2_group(p_m, ssm_conv_w[l], ssm_conv_b[l], ssm_dt_bias[l], ssm_A_log[l],
                           ssm_D[l], ssm_norm_w[l])
        y_f = fox_group(p_f, fox_f_bias[l], fox_norm_w[l])
        mix = jnp.concatenate([y_r, y_m, y_f], axis=-1) @ w_out[l]
        h = h + rms_norm(mix, norm_mix_post[l])
        u = rms_norm(h, norm_mlp_pre[l])
        ff = jnp.square(jax.nn.relu(u @ w_mlp_up[l])) @ w_mlp_down[l]
        h = h + rms_norm(ff, norm_mlp_post[l])
    return h
```

```python
import functools
import math

import jax
import jax.numpy as jnp
from jax import lax
from jax.experimental import pallas as pl
from jax.experimental.pallas import tpu as pltpu

F32 = jnp.float32
BF16 = jnp.bfloat16

HEAD_DIM = 64
NORM_EPS = 1e-6
C_R = 256
R_DECAY, R_AAA, R_GATE = 32, 32, 64
RWKV_GN_EPS = 64e-5
RWKV_COLS = 3 * C_R + R_DECAY + R_AAA + R_GATE
RWKV_CHUNK = 64
C_M = 512
H_M = 8
SSM_STATE = 128
SSM_GROUPS = 2
CONV_K = 4
CONV_DIM = C_M + 2 * SSM_GROUPS * SSM_STATE
SSD_CHUNK = 128
SSM_NORM_EPS = 1e-5
SSM_COLS_PAD = C_M + CONV_DIM + 128
C_F = 256
H_F = 4
FOX_COLS_PAD = 3 * C_F + 128
FOX_BLOCK = 256
LANE = 128
NEG_BIG = -1e30

VMEM_LIMIT = 48 * 1024 * 1024

_NN = (((1,), (0,)), ((), ()))
_NT = (((1,), (1,)), ((), ()))


def _split(x, n):
    if x.dtype == BF16:
        return [x]
    terms = []
    rem = x
    for i in range(n):
        t = rem.astype(BF16)
        terms.append(t)
        if i + 1 < n:
            rem = rem - t.astype(F32)
    return terms


def _mm(a, b, na=1, nb=1, dims=_NN):
    ta, tb = _split(a, na), _split(b, nb)
    depth = max(len(ta), len(tb))
    out = None
    for i, x in enumerate(ta):
        for j, y in enumerate(tb):
            if i + j < depth:
                d = lax.dot_general(x, y, dims, preferred_element_type=F32)
                out = d if out is None else out + d
    return out


def _sigmoid(x):
    return 1.0 / (1.0 + jnp.exp(-x))


def _softplus(x):
    return jnp.maximum(x, 0.0) + jnp.log1p(jnp.exp(-jnp.abs(x)))


def _iota(shape, dim):
    return lax.broadcasted_iota(jnp.int32, shape, dim)


def _block_diag(x, nblk, rblk, cblk):
    t = jnp.concatenate([x] * nblk, axis=0)
    shape = t.shape
    keep = (_iota(shape, 0) // rblk) == (_iota(shape, 1) // cblk)
    return jnp.where(keep, t, 0.0)


def _inproj_kernel(h_ref, g_ref, wr_ref, wm_ref, wf_ref, pr_ref, pm_ref, pf_ref):
    x = h_ref[...]
    ms = jnp.mean(x * x, axis=-1, keepdims=True)
    u = (x * lax.rsqrt(ms + NORM_EPS) * g_ref[...]).astype(BF16)
    pr_ref[...] = jnp.dot(u, wr_ref[...], preferred_element_type=F32)
    pm_ref[...] = jnp.dot(u, wm_ref[...], preferred_element_type=F32)
    pf_ref[...] = jnp.dot(u, wf_ref[...], preferred_element_type=F32)


def _const_spec(shape):
    nd = len(shape)
    return pl.BlockSpec(shape, lambda *_: (0,) * nd, pipeline_mode=pl.Buffered(1))


def _inproj(h2d, g, wr, wm, wf, tm):
    n, d = h2d.shape
    outs = [jax.ShapeDtypeStruct((n, w.shape[1]), F32) for w in (wr, wm, wf)]
    return pl.pallas_call(
        _inproj_kernel,
        grid=(n // tm,),
        in_specs=[pl.BlockSpec((tm, d), lambda i: (i, 0)), _const_spec(g.shape),
                  _const_spec(wr.shape), _const_spec(wm.shape), _const_spec(wf.shape)],
        out_specs=[pl.BlockSpec((tm, w.shape[1]), lambda i: (i, 0)) for w in (wr, wm, wf)],
        out_shape=outs,
        name="inproj",
        compiler_params=pltpu.CompilerParams(
            dimension_semantics=("parallel",), vmem_limit_bytes=VMEM_LIMIT),
    )(h2d, g, wr, wm, wf)


def _rwkv_kernel(p_ref, mu_ref, w0_ref, w2_ref, a0_ref, a2_ref, g2_ref, kk_ref, ka_ref,
                 rk_ref, lnw_ref, lnb_ref, o_ref, s_ref, prev_ref):
    L, C = RWKV_CHUNK, C_R
    nh = C // HEAD_DIM

    @pl.when(pl.program_id(1) == 0)
    def _():
        s_ref[...] = jnp.zeros_like(s_ref)
        prev_ref[...] = jnp.zeros_like(prev_ref)

    p = p_ref[...]
    row = _iota((L, 1), 0)
    shifted = jnp.where(row == 0, prev_ref[7:8, :], pltpu.roll(p, 1, 0))
    prev_ref[...] = p[L - 8:L, :]
    xs = p + (shifted - p) * mu_ref[...]
    r, k, v, lo = xs[:, :C], xs[:, C:2 * C], xs[:, 2 * C:3 * C], xs[:, 3 * C:]

    seg = ((_iota((C, C), 0) // HEAD_DIM) == (_iota((C, C), 1) // HEAD_DIM))
    seg_ones = jnp.where(seg, 1.0, 0.0).astype(BF16)
    seg_mean = jnp.where(seg, 1.0 / HEAD_DIM, 0.0).astype(BF16)

    logw = w0_ref[...] + _mm(jnp.tanh(lo), w2_ref[...], 2, 2)
    lw = -math.exp(-0.5) * _sigmoid(logw)
    lr = _sigmoid(a0_ref[...] + _mm(lo, a2_ref[...], 2, 2))
    gate = _mm(_sigmoid(lo), g2_ref[...], 2, 2)
    kk = k * kk_ref[...]
    kk = kk / jnp.maximum(jnp.sqrt(_mm(kk * kk, seg_ones, 3, 1)), 1e-12)
    k = k * (1.0 + (lr - 1.0) * ka_ref[...])
    a = -kk
    b = kk * lr

    tri = jnp.where(_iota((L, L), 0) >= _iota((L, L), 1), 1.0, 0.0).astype(BF16)
    cum = _mm(tri, lw, 1, 3)
    e_neg = jnp.exp(-cum)
    At = a * jnp.exp(cum - lw)
    Rt = r * jnp.exp(cum)
    Kt = k * e_neg
    Bt = b * e_neg
    cl = cum[L - 1:L, :]
    e_end = jnp.exp(cl - cum)

    bd = functools.partial(_block_diag, nblk=nh, rblk=L, cblk=HEAD_DIM)
    m1 = _mm(jnp.concatenate([At, Rt], axis=0),
             jnp.concatenate([bd(Bt), bd(Kt)], axis=0), 2, 2, _NT)
    tcol = _iota((L, C), 1) % HEAD_DIM
    trow = _iota((L, C), 0)
    strict = tcol < trow
    incl = tcol <= trow
    n_ab = jnp.where(strict, m1[:L, :C], 0.0)
    a_ak = jnp.where(strict, m1[:L, C:], 0.0)
    r_rb = jnp.where(incl, m1[L:, :C], 0.0)
    r_rk = jnp.where(incl, m1[L:, C:], 0.0)

    x = n_ab
    t_inv = jnp.where(tcol == trow, 1.0, 0.0) + x
    for _ in range(int(math.log2(L)) - 1):
        x = _mm(x, bd(x), 2, 2)
        t_inv = t_inv + _mm(t_inv, bd(x), 2, 2)

    s0 = s_ref[...]
    bdv = bd(v)
    rhs = _mm(At, s0, 2, 2, _NT) + _mm(a_ak, bdv, 2, 2)
    u = _mm(t_inv, bd(rhs), 2, 2)
    y = _mm(Rt, s0, 2, 2, _NT) + _mm(r_rb, bd(u), 2, 2) + _mm(r_rk, bdv, 2, 2)
    upd = _mm(jnp.concatenate([u, v], axis=0).T,
              jnp.concatenate([b * e_end, k * e_end], axis=0), 2, 2)
    s_ref[...] = s0 * jnp.exp(cl) + jnp.where(seg, upd, 0.0)

    mean = _mm(y, seg_mean, 3, 1)
    d = y - mean
    var = _mm(d * d, seg_mean, 3, 1)
    yn = d * lax.rsqrt(var + RWKV_GN_EPS) * lnw_ref[...] + lnb_ref[...]
    bonus = _mm(r * k * rk_ref[...], seg_ones, 3, 1) * v
    o_ref[...] = (yn + bonus) * gate


def _rwkv(p_r, prm):
    bsz, t, cols = p_r.shape
    L = RWKV_CHUNK
    names = ("mu", "w0", "w2", "a0", "a2", "g2", "k_k", "k_a", "r_k", "ln_w", "ln_b")
    params = [prm[nm] for nm in names]
    return pl.pallas_call(
        _rwkv_kernel,
        grid=(bsz, t // L),
        in_specs=[pl.BlockSpec((None, L, cols), lambda b, c: (b, c, 0))]
                 + [_const_spec(x.shape) for x in params],
        out_specs=pl.BlockSpec((None, L, C_R), lambda b, c: (b, c, 0)),
        out_shape=jax.ShapeDtypeStruct((bsz, t, C_R), F32),
        scratch_shapes=[pltpu.VMEM((C_R, C_R), F32), pltpu.VMEM((8, cols), F32)],
        name="rwkv7",
        compiler_params=pltpu.CompilerParams(
            dimension_semantics=("parallel", "arbitrary"), vmem_limit_bytes=VMEM_LIMIT),
    )(p_r, *params)


def _ssd_kernel(p_ref, cw_ref, cb_ref, dtb_ref, a_ref, d_ref, nw_ref, o_ref, xe_ref, s_ref):
    L, N = SSD_CHUNK, SSM_STATE
    hg = H_M // SSM_GROUPS
    gw = hg * HEAD_DIM

    @pl.when(pl.program_id(1) == 0)
    def _():
        s_ref[...] = jnp.zeros_like(s_ref)
        xe_ref[0:8, :] = jnp.zeros((8, CONV_DIM), F32)

    @pl.when(pl.program_id(1) > 0)
    def _():
        xe_ref[0:8, :] = xe_ref[L:L + 8, :]

    p = p_ref[...]
    z = p[:, :C_M]
    xbc = p[:, C_M:C_M + CONV_DIM]
    xe_ref[8:8 + L, :] = xbc
    cw = cw_ref[...]
    conv = cb_ref[...] + xbc * cw[CONV_K - 1:CONV_K, :]
    for s in range(1, CONV_K):
        conv = conv + xe_ref[8 - s:8 - s + L, :] * cw[CONV_K - 1 - s:CONV_K - s, :]
    act = conv * _sigmoid(conv)
    xs = act[:, :C_M]
    bm = act[:, C_M:C_M + SSM_GROUPS * N]
    cm = act[:, C_M + SSM_GROUPS * N:]

    dt = _softplus(p[:, C_M + CONV_DIM:] + dtb_ref[...])
    a = dt * a_ref[...]
    tri = jnp.where(_iota((L, L), 0) >= _iota((L, L), 1), 1.0, 0.0).astype(BF16)
    acum = _mm(tri, a, 1, 3)
    exp_c = jnp.where(_iota((LANE, C_M), 0) == _iota((LANE, C_M), 1) // HEAD_DIM, 1.0, 0.0).astype(BF16)
    exp_t = jnp.where(_iota((LANE, H_M * LANE), 0) == _iota((LANE, H_M * LANE), 1) // LANE,
                      1.0, 0.0).astype(BF16)
    acum_x = _mm(acum, exp_c, 3, 1)
    dt_x = _mm(dt, exp_c, 3, 1)
    acum_col = _mm(acum, exp_t, 3, 1)
    acum_row = acum.T
    atot_x = acum_x[L - 1:L, :]
    xdt = xs * dt_x
    x_end = xdt * jnp.exp(atot_x - acum_x)
    e_in = jnp.exp(acum_x)
    e_tot = jnp.exp(atot_x)
    causal = _iota((L, L), 0) >= _iota((L, L), 1)

    s_old = s_ref[...]
    ys = []
    for g in range(SSM_GROUPS):
        bg = bm[:, g * N:(g + 1) * N]
        cg = cm[:, g * N:(g + 1) * N]
        cb = _mm(cg, bg, 1, 1, _NT)
        lmats = []
        for e in range(hg):
            h = g * hg + e
            diff = acum_col[:, h * LANE:(h + 1) * LANE] - acum_row[h:h + 1, :]
            lmats.append(cb * jnp.exp(jnp.where(causal, diff, NEG_BIG)))
        gs = slice(g * gw, (g + 1) * gw)
        y_diag = _mm(jnp.concatenate(lmats, axis=1), _block_diag(xdt[:, gs], hg, L, HEAD_DIM))
        sg = s_old[:, gs]
        y_off = _mm(cg, sg) * e_in[:, gs]
        s_ref[:, gs] = sg * e_tot[:, gs] + _mm(bg.T, x_end[:, gs])
        ys.append(y_diag + y_off)
    y = jnp.concatenate(ys, axis=1) + d_ref[...] * xs
    y = y * (z * _sigmoid(z))
    outs = []
    for g in range(SSM_GROUPS):
        yg = y[:, g * gw:(g + 1) * gw]
        outs.append(yg * lax.rsqrt(jnp.mean(yg * yg, axis=-1, keepdims=True) + SSM_NORM_EPS))
    o_ref[...] = jnp.concatenate(outs, axis=1) * nw_ref[...]


def _ssd(p_m, prm):
    bsz, t, cols = p_m.shape
    L = SSD_CHUNK
    names = ("conv_w", "conv_b", "dt_bias", "a_neg", "d_x", "norm_w")
    params = [prm[nm] for nm in names]
    return pl.pallas_call(
        _ssd_kernel,
        grid=(bsz, t // L),
        in_specs=[pl.BlockSpec((None, L, cols), lambda b, c: (b, c, 0))]
                 + [_const_spec(x.shape) for x in params],
        out_specs=pl.BlockSpec((None, L, C_M), lambda b, c: (b, c, 0)),
        out_shape=jax.ShapeDtypeStruct((bsz, t, C_M), F32),
        scratch_shapes=[pltpu.VMEM((L + 8, CONV_DIM), F32), pltpu.VMEM((SSM_STATE, C_M), F32)],
        name="ssd",
        compiler_params=pltpu.CompilerParams(
            dimension_semantics=("parallel", "arbitrary"), vmem_limit_bytes=VMEM_LIMIT),
    )(p_m, *params)


def _fox_prep_kernel(f_ref, fb_ref, c_ref, ct_ref, carry_ref):
    L = f_ref.shape[0]

    @pl.when(pl.program_id(1) == 0)
    def _():
        carry_ref[...] = jnp.zeros_like(carry_ref)

    log_f = -_softplus(-(f_ref[...] + fb_ref[...]))
    tri = jnp.where(_iota((L, L), 0) >= _iota((L, L), 1), 1.0, 0.0).astype(BF16)
    c = _mm(tri, log_f, 1, 3) + carry_ref[0:1, :]
    carry_ref[...] = jnp.broadcast_to(c[L - 1:L, :], carry_ref.shape)
    c_ref[...] = c
    ct_ref[...] = c.T[0:8, :]


def _fox_prep(p_f, f_bias):
    bsz, t, cols = p_f.shape
    L = LANE
    fblk = (cols - LANE) // LANE
    return pl.pallas_call(
        _fox_prep_kernel,
        grid=(bsz, t // L),
        in_specs=[pl.BlockSpec((None, L, LANE), lambda b, c: (b, c, fblk)), _const_spec(f_bias.shape)],
        out_specs=[pl.BlockSpec((None, L, LANE), lambda b, c: (b, c, 0)),
                   pl.BlockSpec((None, 8, L), lambda b, c: (b, 0, c))],
        out_shape=[jax.ShapeDtypeStruct((bsz, t, LANE), F32), jax.ShapeDtypeStruct((bsz, 8, t), F32)],
        scratch_shapes=[pltpu.VMEM((8, LANE), F32)],
        name="fox_prep",
        compiler_params=pltpu.CompilerParams(
            dimension_semantics=("parallel", "arbitrary"), vmem_limit_bytes=VMEM_LIMIT),
    )(p_f, f_bias)


def _fox_kernel(q_ref, k_ref, v_ref, cq_ref, ck_ref, nw_ref, o_ref, m_ref, l_ref, acc_ref):
    i, j = pl.program_id(1), pl.program_id(2)
    tq, tk = q_ref.shape[0], k_ref.shape[0]
    scale = 1.0 / math.sqrt(HEAD_DIM)

    @pl.when(j == 0)
    def _():
        m_ref[...] = jnp.full(m_ref.shape, NEG_BIG, F32)
        l_ref[...] = jnp.zeros_like(l_ref)
        acc_ref[...] = jnp.zeros_like(acc_ref)

    @pl.when(j <= i)
    def _():
        q, k, v = q_ref[...], k_ref[...], v_ref[...]
        cq, ck = cq_ref[...], ck_ref[...]
        visible = (i * tq + _iota((tq, tk), 0)) >= (j * tk + _iota((tq, tk), 1))
        for h in range(H_F):
            hs = slice(h * HEAD_DIM, (h + 1) * HEAD_DIM)
            s = _mm(q[:, hs], k[:, hs], 1, 1, _NT) * scale
            s = s + (cq[:, h:h + 1] - ck[h:h + 1, :])
            s = jnp.where(visible, s, NEG_BIG)
            m_old = m_ref[h]
            m_new = jnp.maximum(m_old, jnp.max(s, axis=-1, keepdims=True))
            alpha = jnp.exp(m_old - m_new)
            pr = jnp.exp(s - m_new)
            l_ref[h] = alpha * l_ref[h] + jnp.sum(pr, axis=-1, keepdims=True)
            acc_ref[h] = alpha * acc_ref[h] + _mm(pr, v[:, hs])
            m_ref[h] = m_new

    @pl.when(j == i)
    def _():
        nw = nw_ref[...]
        for h in range(H_F):
            hs = slice(h * HEAD_DIM, (h + 1) * HEAD_DIM)
            y = acc_ref[h] / l_ref[h]
            y = y * lax.rsqrt(jnp.mean(y * y, axis=-1, keepdims=True) + NORM_EPS)
            o_ref[:, hs] = y * nw[:, hs]


def _fox(p_f, c, ct, norm_w):
    bsz, t, _ = p_f.shape
    blk = FOX_BLOCK
    nb = t // blk
    kv = lambda col: (lambda b, i, j: (b, jnp.minimum(j, i), col))
    return pl.pallas_call(
        _fox_kernel,
        grid=(bsz, nb, nb),
        in_specs=[pl.BlockSpec((None, blk, C_F), lambda b, i, j: (b, i, 0)),
                  pl.BlockSpec((None, blk, C_F), kv(1)),
                  pl.BlockSpec((None, blk, C_F), kv(2)),
                  pl.BlockSpec((None, blk, LANE), lambda b, i, j: (b, i, 0)),
                  pl.BlockSpec((None, 8, blk), lambda b, i, j: (b, 0, jnp.minimum(j, i))),
                  _const_spec(norm_w.shape)],
        out_specs=pl.BlockSpec((None, blk, C_F), lambda b, i, j: (b, i, 0)),
        out_shape=jax.ShapeDtypeStruct((bsz, t, C_F), F32),
        scratch_shapes=[pltpu.VMEM((H_F, blk, 1), F32), pltpu.VMEM((H_F, blk, 1), F32),
                        pltpu.VMEM((H_F, blk, HEAD_DIM), F32)],
        name="fox_attn",
        compiler_params=pltpu.CompilerParams(
            dimension_semantics=("parallel", "parallel", "arbitrary"), vmem_limit_bytes=VMEM_LIMIT),
    )(p_f, p_f, p_f, c, ct, norm_w)


def _rms(x, g):
    return x * lax.rsqrt(jnp.mean(x * x, axis=-1, keepdims=True) + NORM_EPS) * g


def _mix_mlp_kernel(h_ref, yr_ref, ym_ref, yf_ref, wor_ref, wom_ref, wof_ref, g1_ref, g2_ref, g3_ref,
                    wu_ref, wd_ref, o_ref):
    mix = (jnp.dot(yr_ref[...].astype(BF16), wor_ref[...], preferred_element_type=F32)
           + jnp.dot(ym_ref[...].astype(BF16), wom_ref[...], preferred_element_type=F32)
           + jnp.dot(yf_ref[...].astype(BF16), wof_ref[...], preferred_element_type=F32))
    h1 = h_ref[...] + _rms(mix, g1_ref[...])
    u = _rms(h1, g2_ref[...]).astype(BF16)
    d_ff = wu_ref.shape[1]
    step = 1024
    ff = None
    for c0 in range(0, d_ff, step):
        hid = jnp.dot(u, wu_ref[:, c0:c0 + step], preferred_element_type=F32)
        hid = jnp.square(jnp.maximum(hid, 0.0)).astype(BF16)
        part = jnp.dot(hid, wd_ref[c0:c0 + step, :], preferred_element_type=F32)
        ff = part if ff is None else ff + part
    o_ref[...] = h1 + _rms(ff, g3_ref[...])


def _mix_mlp(h2d, yr, ym, yf, prm, tm):
    n, d = h2d.shape
    names = ("wo_r", "wo_m", "wo_f", "g_mix_post", "g_mlp_pre", "g_mlp_post", "w_up", "w_down")
    params = [prm[nm] for nm in names]
    row = lambda w: pl.BlockSpec((tm, w), lambda i: (i, 0))
    return pl.pallas_call(
        _mix_mlp_kernel,
        grid=(n // tm,),
        in_specs=[row(d), row(yr.shape[1]), row(ym.shape[1]), row(yf.shape[1])]
                 + [_const_spec(x.shape) for x in params],
        out_specs=row(d),
        out_shape=jax.ShapeDtypeStruct((n, d), F32),
        name="mix_mlp",
        compiler_params=pltpu.CompilerParams(
            dimension_semantics=("parallel",), vmem_limit_bytes=VMEM_LIMIT),
    )(h2d, yr, ym, yf, *params)


def _pad_cols(w, width):
    return jnp.pad(w, ((0, 0), (0, width - w.shape[1])))


def _layer_params(l, norm_mix_pre, norm_mix_post, w_in, rwkv_mu, rwkv_w0, rwkv_w2, rwkv_a0, rwkv_a2,
                  rwkv_g2, rwkv_k_k, rwkv_k_a, rwkv_r_k, rwkv_ln_w, rwkv_ln_b, ssm_conv_w, ssm_conv_b,
                  ssm_dt_bias, ssm_A_log, ssm_D, ssm_norm_w, fox_f_bias, fox_norm_w, w_out,
                  norm_mlp_pre, norm_mlp_post, w_mlp_up, w_mlp_down):
    row = lambda x: x.reshape(1, -1).astype(F32)
    ssm_cols = C_M + CONV_DIM + H_M
    w = w_in[l]
    w_r = w[:, :RWKV_COLS]
    w_m = _pad_cols(w[:, RWKV_COLS:RWKV_COLS + ssm_cols], SSM_COLS_PAD)
    w_f = _pad_cols(w[:, RWKV_COLS + ssm_cols:], FOX_COLS_PAD)
    lo_rows = R_DECAY + R_AAA + R_GATE
    place = lambda m, r0: jnp.pad(m, ((r0, lo_rows - r0 - m.shape[0]), (0, 0)))
    inproj = dict(g=row(norm_mix_pre[l]), w_r=w_r.astype(BF16), w_m=w_m.astype(BF16), w_f=w_f.astype(BF16))
    rwkv = dict(mu=row(rwkv_mu[l]), w0=row(rwkv_w0[l]), w2=place(rwkv_w2[l], 0), a0=row(rwkv_a0[l]),
                a2=place(rwkv_a2[l], R_DECAY), g2=place(rwkv_g2[l], R_DECAY + R_AAA),
                k_k=row(rwkv_k_k[l]), k_a=row(rwkv_k_a[l]), r_k=row(rwkv_r_k[l]),
                ln_w=row(rwkv_ln_w[l]), ln_b=row(rwkv_ln_b[l]))
    ssd = dict(conv_w=ssm_conv_w[l], conv_b=row(ssm_conv_b[l]),
               dt_bias=_pad_cols(row(ssm_dt_bias[l]), LANE),
               a_neg=_pad_cols(row(-jnp.exp(ssm_A_log[l].astype(F32))), LANE),
               d_x=row(jnp.repeat(ssm_D[l], HEAD_DIM)), norm_w=row(ssm_norm_w[l]))
    fox = dict(f_bias=_pad_cols(row(fox_f_bias[l]), LANE), norm_w=row(fox_norm_w[l]))
    wo = w_out[l].astype(BF16)
    mlp = dict(wo_r=wo[:C_R], wo_m=wo[C_R:C_R + C_M], wo_f=wo[C_R + C_M:],
               g_mix_post=row(norm_mix_post[l]), g_mlp_pre=row(norm_mlp_pre[l]),
               g_mlp_post=row(norm_mlp_post[l]), w_up=w_mlp_up[l].astype(BF16),
               w_down=w_mlp_down[l].astype(BF16))
    return inproj, rwkv, ssd, fox, mlp


def kernel(x, norm_mix_pre, norm_mix_post, w_in, rwkv_mu, rwkv_w0, rwkv_w2, rwkv_a0, rwkv_a2, rwkv_g2, rwkv_k_k, rwkv_k_a, rwkv_r_k, rwkv_ln_w, rwkv_ln_b, ssm_conv_w, ssm_conv_b, ssm_dt_bias, ssm_A_log, ssm_D, ssm_norm_w, fox_f_bias, fox_norm_w, w_out, norm_mlp_pre, norm_mlp_post, w_mlp_up, w_mlp_down):
    weights = (norm_mix_pre, norm_mix_post, w_in, rwkv_mu, rwkv_w0, rwkv_w2, rwkv_a0, rwkv_a2, rwkv_g2,
               rwkv_k_k, rwkv_k_a, rwkv_r_k, rwkv_ln_w, rwkv_ln_b, ssm_conv_w, ssm_conv_b, ssm_dt_bias,
               ssm_A_log, ssm_D, ssm_norm_w, fox_f_bias, fox_norm_w, w_out, norm_mlp_pre, norm_mlp_post,
               w_mlp_up, w_mlp_down)
    bsz, t, d = x.shape
    n = bsz * t
    assert t % FOX_BLOCK == 0 and t % SSD_CHUNK == 0 and t % RWKV_CHUNK == 0
    tm = 512 if n % 512 == 0 else 256
    h = x.reshape(n, d)
    for l in range(w_in.shape[0]):
        inproj, rwkv, ssd, fox, mlp = _layer_params(l, *weights)
        p_r, p_m, p_f = _inproj(h, inproj["g"], inproj["w_r"], inproj["w_m"], inproj["w_f"], tm)
        y_r = _rwkv(p_r.reshape(bsz, t, -1), rwkv)
        y_m = _ssd(p_m.reshape(bsz, t, -1), ssd)
        p_f3 = p_f.reshape(bsz, t, -1)
        c, ct = _fox_prep(p_f3, fox["f_bias"])
        y_f = _fox(p_f3, c, ct, fox["norm_w"])
        h = _mix_mlp(h, y_r.reshape(n, -1), y_m.reshape(n, -1), y_f.reshape(n, -1), mlp, tm)
    return h.reshape(bsz, t, d)
```

```python
import functools
import math

import jax
import jax.numpy as jnp
from jax import lax
from jax.experimental import pallas as pl
from jax.experimental.pallas import tpu as pltpu

F32 = jnp.float32
BF16 = jnp.bfloat16

HEAD_DIM = 64
NORM_EPS = 1e-6
C_R = 256
R_DECAY, R_AAA, R_GATE = 32, 32, 64
RWKV_GN_EPS = 64e-5
RWKV_COLS = 3 * C_R + R_DECAY + R_AAA + R_GATE
RWKV_CHUNK = 64
RWKV_ROWS = 4
RWKV_PASSES = dict(decay=(2, 2), lowrank=(1, 1), seg=(1, 1), m1=(1, 1), inv=(1, 1), u=(1, 1), y=(1, 1),
                   s=(1, 1))
C_M = 512
H_M = 8
SSM_STATE = 128
SSM_GROUPS = 2
CONV_K = 4
CONV_DIM = C_M + 2 * SSM_GROUPS * SSM_STATE
SSD_CHUNK = 128
SSM_NORM_EPS = 1e-5
SSM_COLS_PAD = C_M + CONV_DIM + 128
C_F = 256
H_F = 4
FOX_COLS_PAD = 3 * C_F + 128
FOX_BLOCK = 512
LOG2E = 1.4426950408889634
LANE = 128
NEG_BIG = -1e30

VMEM_LIMIT = 48 * 1024 * 1024

_NN = (((1,), (0,)), ((), ()))
_NT = (((1,), (1,)), ((), ()))


def _split(x, n):
    if x.dtype == BF16:
        return [x]
    terms = []
    rem = x
    for i in range(n):
        t = rem.astype(BF16)
        terms.append(t)
        if i + 1 < n:
            rem = rem - t.astype(F32)
    return terms


def _mm(a, b, na=1, nb=1, dims=_NN):
    ta, tb = _split(a, na), _split(b, nb)
    depth = max(len(ta), len(tb))
    out = None
    for i, x in enumerate(ta):
        for j, y in enumerate(tb):
            if i + j < depth:
                d = lax.dot_general(x, y, dims, preferred_element_type=F32)
                out = d if out is None else out + d
    return out


def _sigmoid(x):
    return 1.0 / (1.0 + jnp.exp(-x))


def _softplus(x):
    return jnp.maximum(x, 0.0) + jnp.log1p(jnp.exp(-jnp.abs(x)))


def _iota(shape, dim):
    return lax.broadcasted_iota(jnp.int32, shape, dim)


def _block_diag(x, nblk, rblk, cblk):
    t = jnp.concatenate([x] * nblk, axis=0)
    shape = t.shape
    keep = (_iota(shape, 0) // rblk) == (_iota(shape, 1) // cblk)
    return jnp.where(keep, t, 0.0)


def _inproj_kernel(h_ref, g_ref, wr_ref, wm_ref, wf_ref, pr_ref, pm_ref, pf_ref):
    x = h_ref[...]
    ms = jnp.mean(x * x, axis=-1, keepdims=True)
    u = (x * lax.rsqrt(ms + NORM_EPS) * g_ref[...]).astype(BF16)
    pr_ref[...] = jnp.dot(u, wr_ref[...], preferred_element_type=F32)
    pm_ref[...] = jnp.dot(u, wm_ref[...], preferred_element_type=F32)
    pf_ref[...] = jnp.dot(u, wf_ref[...], preferred_element_type=F32)


def _const_spec(shape):
    nd = len(shape)
    return pl.BlockSpec(shape, lambda *_: (0,) * nd, pipeline_mode=pl.Buffered(1))


def _inproj(h2d, g, wr, wm, wf, tm):
    n, d = h2d.shape
    outs = [jax.ShapeDtypeStruct((n, w.shape[1]), F32) for w in (wr, wm, wf)]
    return pl.pallas_call(
        _inproj_kernel,
        grid=(n // tm,),
        in_specs=[pl.BlockSpec((tm, d), lambda i: (i, 0)), _const_spec(g.shape),
                  _const_spec(wr.shape), _const_spec(wm.shape), _const_spec(wf.shape)],
        out_specs=[pl.BlockSpec((tm, w.shape[1]), lambda i: (i, 0)) for w in (wr, wm, wf)],
        out_shape=outs,
        name="inproj",
        compiler_params=pltpu.CompilerParams(
            dimension_semantics=("parallel",), vmem_limit_bytes=VMEM_LIMIT),
    )(h2d, g, wr, wm, wf)


def _rwkv_kernel(p_ref, mu_ref, w0_ref, w2_ref, a0_ref, a2_ref, g2_ref, kk_ref, ka_ref,
                 rk_ref, lnw_ref, lnb_ref, o_ref, s_ref, prev_ref):
    nb, L, C = p_ref.shape[0], RWKV_CHUNK, C_R
    nh = C // HEAD_DIM
    R = nb * L
    P = RWKV_PASSES

    @pl.when(pl.program_id(1) == 0)
    def _():
        s_ref[...] = jnp.zeros_like(s_ref)
        prev_ref[...] = jnp.zeros_like(prev_ref)

    p = jnp.concatenate([p_ref[bi] for bi in range(nb)], axis=0)
    row = _iota((R, 1), 0)
    shifted = pltpu.roll(p, 1, 0)
    for bi in range(nb):
        shifted = jnp.where(row == bi * L, prev_ref[bi, 7:8, :], shifted)
        prev_ref[bi] = p_ref[bi, L - 8:L, :]
    xs = p + (shifted - p) * mu_ref[...]
    r, k, v, lo = xs[:, :C], xs[:, C:2 * C], xs[:, 2 * C:3 * C], xs[:, 3 * C:]

    seg = ((_iota((C, C), 0) // HEAD_DIM) == (_iota((C, C), 1) // HEAD_DIM))
    seg_ones = jnp.where(seg, 1.0, 0.0).astype(BF16)
    seg_mean = jnp.where(seg, 1.0 / HEAD_DIM, 0.0).astype(BF16)

    logw = w0_ref[...] + _mm(jnp.tanh(lo), w2_ref[...], *P["decay"])
    lw = -math.exp(-0.5) * _sigmoid(logw)
    lr = _sigmoid(a0_ref[...] + _mm(lo, a2_ref[...], *P["lowrank"]))
    gate = _mm(_sigmoid(lo), g2_ref[...], *P["lowrank"])
    kk = k * kk_ref[...]
    kk = kk / jnp.maximum(jnp.sqrt(_mm(kk * kk, seg_ones, *P["seg"])), 1e-12)
    k = k * (1.0 + (lr - 1.0) * ka_ref[...])
    a = -kk
    b = kk * lr

    rr, cc = _iota((R, R), 0), _iota((R, R), 1)
    tri = jnp.where((rr >= cc) & (rr // L == cc // L), 1.0, 0.0).astype(BF16)
    cum = _mm(tri, lw, 1, 3)
    cl = jnp.concatenate([jnp.broadcast_to(cum[(bi + 1) * L - 1:(bi + 1) * L, :], (L, C))
                          for bi in range(nb)], axis=0)
    e_neg = jnp.exp(-cum)
    e_end = jnp.exp(cl - cum)
    chunks = lambda x: [x[bi * L:(bi + 1) * L] for bi in range(nb)]
    At, Rt, Kt, Bt = chunks(a * jnp.exp(cum - lw)), chunks(r * jnp.exp(cum)), chunks(k * e_neg), chunks(b * e_neg)
    Bp, Kp, vs = chunks(b * e_end), chunks(k * e_end), chunks(v)
    e_tot = [x[L - 1:L, :] for x in chunks(jnp.exp(cl))]

    bd = functools.partial(_block_diag, nblk=nh, rblk=L, cblk=HEAD_DIM)
    rows = range(nb)
    m1 = [_mm(jnp.concatenate([At[i], Rt[i]], axis=0),
              jnp.concatenate([bd(Bt[i]), bd(Kt[i])], axis=0), *P["m1"], _NT) for i in rows]
    tcol = _iota((L, C), 1) % HEAD_DIM
    trow = _iota((L, C), 0)
    strict = tcol < trow
    incl = tcol <= trow
    a_ak = [jnp.where(strict, m[:L, C:], 0.0) for m in m1]
    r_rb = [jnp.where(incl, m[L:, :C], 0.0) for m in m1]
    r_rk = [jnp.where(incl, m[L:, C:], 0.0) for m in m1]

    x = [jnp.where(strict, m[:L, :C], 0.0) for m in m1]
    eye = jnp.where(tcol == trow, 1.0, 0.0)
    t_inv = [eye + xi for xi in x]
    for _ in range(int(math.log2(L)) - 1):
        x = [_mm(xi, bd(xi), *P["inv"]) for xi in x]
        t_inv = [ti + _mm(ti, bd(xi), *P["inv"]) for ti, xi in zip(t_inv, x)]

    s0 = [s_ref[i] for i in rows]
    bdv = [bd(vi) for vi in vs]
    rhs = [_mm(At[i], s0[i], *P["u"], _NT) + _mm(a_ak[i], bdv[i], *P["u"]) for i in rows]
    u = [_mm(t_inv[i], bd(rhs[i]), *P["u"]) for i in rows]
    y = [_mm(Rt[i], s0[i], *P["y"], _NT) + _mm(r_rb[i], bd(u[i]), *P["y"]) + _mm(r_rk[i], bdv[i], *P["y"])
         for i in rows]
    upd = [_mm(jnp.concatenate([u[i], vs[i]], axis=0).T,
               jnp.concatenate([Bp[i], Kp[i]], axis=0), *P["s"]) for i in rows]
    for i in rows:
        s_ref[i] = s0[i] * e_tot[i] + jnp.where(seg, upd[i], 0.0)

    y = jnp.concatenate(y, axis=0)
    mean = _mm(y, seg_mean, *P["seg"])
    d = y - mean
    var = _mm(d * d, seg_mean, *P["seg"])
    yn = d * lax.rsqrt(var + RWKV_GN_EPS) * lnw_ref[...] + lnb_ref[...]
    bonus = _mm(r * k * rk_ref[...], seg_ones, *P["seg"]) * v
    out = (yn + bonus) * gate
    for i in rows:
        o_ref[i] = out[i * L:(i + 1) * L]


def _rwkv(p_r, prm):
    bsz, t, cols = p_r.shape
    L = RWKV_CHUNK
    nb = RWKV_ROWS if bsz % RWKV_ROWS == 0 else 1
    names = ("mu", "w0", "w2", "a0", "a2", "g2", "k_k", "k_a", "r_k", "ln_w", "ln_b")
    params = [prm[nm] for nm in names]
    return pl.pallas_call(
        _rwkv_kernel,
        grid=(bsz // nb, t // L),
        in_specs=[pl.BlockSpec((nb, L, cols), lambda b, c: (b, c, 0))]
                 + [_const_spec(x.shape) for x in params],
        out_specs=pl.BlockSpec((nb, L, C_R), lambda b, c: (b, c, 0)),
        out_shape=jax.ShapeDtypeStruct((bsz, t, C_R), F32),
        scratch_shapes=[pltpu.VMEM((nb, C_R, C_R), F32), pltpu.VMEM((nb, 8, cols), F32)],
        name="rwkv7",
        compiler_params=pltpu.CompilerParams(
            dimension_semantics=("parallel", "arbitrary"), vmem_limit_bytes=VMEM_LIMIT),
    )(p_r, *params)


def _ssd_kernel(p_ref, cw_ref, cb_ref, dtb_ref, a_ref, d_ref, nw_ref, o_ref, xe_ref, s_ref):
    L, N = SSD_CHUNK, SSM_STATE
    hg = H_M // SSM_GROUPS
    gw = hg * HEAD_DIM

    @pl.when(pl.program_id(1) == 0)
    def _():
        s_ref[...] = jnp.zeros_like(s_ref)
        xe_ref[0:8, :] = jnp.zeros((8, CONV_DIM), F32)

    @pl.when(pl.program_id(1) > 0)
    def _():
        xe_ref[0:8, :] = xe_ref[L:L + 8, :]

    p = p_ref[...]
    z = p[:, :C_M]
    xbc = p[:, C_M:C_M + CONV_DIM]
    xe_ref[8:8 + L, :] = xbc
    cw = cw_ref[...]
    conv = cb_ref[...] + xbc * cw[CONV_K - 1:CONV_K, :]
    for s in range(1, CONV_K):
        conv = conv + xe_ref[8 - s:8 - s + L, :] * cw[CONV_K - 1 - s:CONV_K - s, :]
    act = conv * _sigmoid(conv)
    xs = act[:, :C_M]
    bm = act[:, C_M:C_M + SSM_GROUPS * N]
    cm = act[:, C_M + SSM_GROUPS * N:]

    dt = _softplus(p[:, C_M + CONV_DIM:] + dtb_ref[...])
    a = dt * a_ref[...]
    tri = jnp.where(_iota((L, L), 0) >= _iota((L, L), 1), 1.0, 0.0).astype(BF16)
    acum = _mm(tri, a, 1, 3)
    exp_c = jnp.where(_iota((LANE, C_M), 0) == _iota((LANE, C_M), 1) // HEAD_DIM, 1.0, 0.0).astype(BF16)
    exp_t = jnp.where(_iota((LANE, H_M * LANE), 0) == _iota((LANE, H_M * LANE), 1) // LANE,
                      1.0, 0.0).astype(BF16)
    acum_x = _mm(acum, exp_c, 3, 1)
    dt_x = _mm(dt, exp_c, 3, 1)
    acum_col = _mm(acum, exp_t, 3, 1)
    acum_row = acum.T
    atot_x = acum_x[L - 1:L, :]
    xdt = xs * dt_x
    x_end = xdt * jnp.exp(atot_x - acum_x)
    e_in = jnp.exp(acum_x)
    e_tot = jnp.exp(atot_x)
    causal = _iota((L, L), 0) >= _iota((L, L), 1)

    s_old = s_ref[...]
    ys = []
    for g in range(SSM_GROUPS):
        bg = bm[:, g * N:(g + 1) * N]
        cg = cm[:, g * N:(g + 1) * N]
        cb = _mm(cg, bg, 1, 1, _NT)
        lmats = []
        for e in range(hg):
            h = g * hg + e
            diff = acum_col[:, h * LANE:(h + 1) * LANE] - acum_row[h:h + 1, :]
            lmats.append(cb * jnp.exp(jnp.where(causal, diff, NEG_BIG)))
        gs = slice(g * gw, (g + 1) * gw)
        y_diag = _mm(jnp.concatenate(lmats, axis=1), _block_diag(xdt[:, gs], hg, L, HEAD_DIM))
        sg = s_old[:, gs]
        y_off = _mm(cg, sg) * e_in[:, gs]
        s_ref[:, gs] = sg * e_tot[:, gs] + _mm(bg.T, x_end[:, gs])
        ys.append(y_diag + y_off)
    y = jnp.concatenate(ys, axis=1) + d_ref[...] * xs
    y = y * (z * _sigmoid(z))
    outs = []
    for g in range(SSM_GROUPS):
        yg = y[:, g * gw:(g + 1) * gw]
        outs.append(yg * lax.rsqrt(jnp.mean(yg * yg, axis=-1, keepdims=True) + SSM_NORM_EPS))
    o_ref[...] = jnp.concatenate(outs, axis=1) * nw_ref[...]


def _ssd(p_m, prm):
    bsz, t, cols = p_m.shape
    L = SSD_CHUNK
    names = ("conv_w", "conv_b", "dt_bias", "a_neg", "d_x", "norm_w")
    params = [prm[nm] for nm in names]
    return pl.pallas_call(
        _ssd_kernel,
        grid=(bsz, t // L),
        in_specs=[pl.BlockSpec((None, L, cols), lambda b, c: (b, c, 0))]
                 + [_const_spec(x.shape) for x in params],
        out_specs=pl.BlockSpec((None, L, C_M), lambda b, c: (b, c, 0)),
        out_shape=jax.ShapeDtypeStruct((bsz, t, C_M), F32),
        scratch_shapes=[pltpu.VMEM((L + 8, CONV_DIM), F32), pltpu.VMEM((SSM_STATE, C_M), F32)],
        name="ssd",
        compiler_params=pltpu.CompilerParams(
            dimension_semantics=("parallel", "arbitrary"), vmem_limit_bytes=VMEM_LIMIT),
    )(p_m, *params)


def _fox_prep_kernel(f_ref, fb_ref, nck_ref, carry_ref):
    L = f_ref.shape[0]

    @pl.when(pl.program_id(1) == 0)
    def _():
        carry_ref[...] = jnp.zeros_like(carry_ref)

    log_f = -_softplus(-(f_ref[...] + fb_ref[...]))
    tri = jnp.where(_iota((L, L), 0) >= _iota((L, L), 1), 1.0, 0.0).astype(BF16)
    c = _mm(tri, log_f, 1, 3) + carry_ref[0:1, :]
    carry_ref[...] = jnp.broadcast_to(c[L - 1:L, :], carry_ref.shape)
    nck_ref[...] = (c * (-LOG2E)).T[0:8, :]


def _fox_prep(p_f, f_bias):
    bsz, t, cols = p_f.shape
    L = FOX_BLOCK
    fblk = (cols - LANE) // LANE
    return pl.pallas_call(
        _fox_prep_kernel,
        grid=(bsz, t // L),
        in_specs=[pl.BlockSpec((None, L, LANE), lambda b, c: (b, c, fblk)), _const_spec(f_bias.shape)],
        out_specs=pl.BlockSpec((None, 8, L), lambda b, c: (b, 0, c)),
        out_shape=jax.ShapeDtypeStruct((bsz, 8, t), F32),
        scratch_shapes=[pltpu.VMEM((8, LANE), F32)],
        name="fox_prep",
        compiler_params=pltpu.CompilerParams(
            dimension_semantics=("parallel", "arbitrary"), vmem_limit_bytes=VMEM_LIMIT),
    )(p_f, f_bias)


def _fox_kernel(q_ref, k_ref, v_ref, nck_ref, nw_ref, o_ref, q2_ref, m_ref, acc_ref):
    i, j = pl.program_id(1), pl.program_id(2)
    tq, tk = q_ref.shape[0], k_ref.shape[0]
    C = C_F

    @pl.when(j == 0)
    def _():
        q2_ref[...] = (q_ref[...] * (LOG2E / math.sqrt(HEAD_DIM))).astype(BF16)
        m_ref[...] = jnp.full(m_ref.shape, NEG_BIG, F32)
        acc_ref[...] = jnp.zeros_like(acc_ref)

    def step(masked):
        kb = jnp.concatenate([k_ref[...].astype(BF16)] * H_F, axis=0)
        vb = jnp.concatenate([v_ref[...].astype(BF16)] * H_F, axis=0)
        keep = (_iota((H_F * tk, C), 0) // tk) == (_iota((H_F * tk, C), 1) // HEAD_DIM)
        zero = jnp.zeros((), BF16)
        kbd = jnp.where(keep, kb, zero)
        rhs = jnp.concatenate([jnp.where(keep, vb, zero), jnp.where(keep, 1.0, 0.0).astype(BF16)], axis=1)
        s = lax.dot_general(q2_ref[...], kbd, _NT, preferred_element_type=F32)
        nck = nck_ref[...]
        if masked:
            visible = _iota((tq, tk), 0) >= _iota((tq, tk), 1)
        ps, alphas = [], []
        for h in range(H_F):
            sh = s[:, h * tk:(h + 1) * tk] + nck[h:h + 1, :]
            if masked:
                sh = jnp.where(visible, sh, NEG_BIG)
            m_old = m_ref[h]
            m_new = jnp.maximum(m_old, jnp.max(sh, axis=-1, keepdims=True))
            m_ref[h] = m_new
            alphas.append(jnp.exp2(m_old - m_new))
            ps.append(jnp.exp2(sh - jnp.concatenate([m_new] * (tk // LANE), axis=1)).astype(BF16))
        first = _iota((tq, LANE), 1) < HEAD_DIM
        alpha = jnp.concatenate([jnp.where(first, alphas[0], alphas[1]),
                                 jnp.where(first, alphas[2], alphas[3])] * 2, axis=1)
        acc_ref[...] = acc_ref[...] * alpha + jnp.dot(jnp.concatenate(ps, axis=1), rhs,
                                                      preferred_element_type=F32)

    @pl.when(j < i)
    def _():
        step(False)

    @pl.when(j == i)
    def _():
        step(True)
        acc = acc_ref[...]
        y = acc[:, :C] / acc[:, C:]
        seg = ((_iota((C, C), 0) // HEAD_DIM) == (_iota((C, C), 1) // HEAD_DIM))
        ms = _mm(y * y, jnp.where(seg, 1.0 / HEAD_DIM, 0.0).astype(BF16), 3, 1)
        o_ref[...] = y * lax.rsqrt(ms + NORM_EPS) * nw_ref[...]


def _fox(p_f, nck, norm_w):
    bsz, t, _ = p_f.shape
    blk = FOX_BLOCK
    nb = t // blk
    kv = lambda col: (lambda b, i, j: (b, jnp.minimum(j, i), col))
    return pl.pallas_call(
        _fox_kernel,
        grid=(bsz, nb, nb),
        in_specs=[pl.BlockSpec((None, blk, C_F), lambda b, i, j: (b, i, 0)),
                  pl.BlockSpec((None, blk, C_F), kv(1)),
                  pl.BlockSpec((None, blk, C_F), kv(2)),
                  pl.BlockSpec((None, 8, blk), lambda b, i, j: (b, 0, jnp.minimum(j, i))),
                  _const_spec(norm_w.shape)],
        out_specs=pl.BlockSpec((None, blk, C_F), lambda b, i, j: (b, i, 0)),
        out_shape=jax.ShapeDtypeStruct((bsz, t, C_F), F32),
        scratch_shapes=[pltpu.VMEM((blk, C_F), BF16), pltpu.VMEM((H_F, blk, LANE), F32),
                        pltpu.VMEM((blk, 2 * C_F), F32)],
        name="fox_attn",
        compiler_params=pltpu.CompilerParams(
            dimension_semantics=("parallel", "parallel", "arbitrary"), vmem_limit_bytes=VMEM_LIMIT),
    )(p_f, p_f, p_f, nck, norm_w)


def _rms(x, g):
    return x * lax.rsqrt(jnp.mean(x * x, axis=-1, keepdims=True) + NORM_EPS) * g


def _mix_mlp_kernel(h_ref, yr_ref, ym_ref, yf_ref, wor_ref, wom_ref, wof_ref, g1_ref, g2_ref, g3_ref,
                    wu_ref, wd_ref, o_ref):
    mix = (jnp.dot(yr_ref[...].astype(BF16), wor_ref[...], preferred_element_type=F32)
           + jnp.dot(ym_ref[...].astype(BF16), wom_ref[...], preferred_element_type=F32)
           + jnp.dot(yf_ref[...].astype(BF16), wof_ref[...], preferred_element_type=F32))
    h1 = h_ref[...] + _rms(mix, g1_ref[...])
    u = _rms(h1, g2_ref[...]).astype(BF16)
    d_ff = wu_ref.shape[1]
    step = 1024
    ff = None
    for c0 in range(0, d_ff, step):
        hid = jnp.dot(u, wu_ref[:, c0:c0 + step], preferred_element_type=F32)
        hid = jnp.square(jnp.maximum(hid, 0.0)).astype(BF16)
        part = jnp.dot(hid, wd_ref[c0:c0 + step, :], preferred_element_type=F32)
        ff = part if ff is None else ff + part
    o_ref[...] = h1 + _rms(ff, g3_ref[...])


def _mix_mlp(h2d, yr, ym, yf, prm, tm):
    n, d = h2d.shape
    names = ("wo_r", "wo_m", "wo_f", "g_mix_post", "g_mlp_pre", "g_mlp_post", "w_up", "w_down")
    params = [prm[nm] for nm in names]
    row = lambda w: pl.BlockSpec((tm, w), lambda i: (i, 0))
    return pl.pallas_call(
        _mix_mlp_kernel,
        grid=(n // tm,),
        in_specs=[row(d), row(yr.shape[1]), row(ym.shape[1]), row(yf.shape[1])]
                 + [_const_spec(x.shape) for x in params],
        out_specs=row(d),
        out_shape=jax.ShapeDtypeStruct((n, d), F32),
        name="mix_mlp",
        compiler_params=pltpu.CompilerParams(
            dimension_semantics=("parallel",), vmem_limit_bytes=VMEM_LIMIT),
    )(h2d, yr, ym, yf, *params)


def _pad_cols(w, width):
    return jnp.pad(w, ((0, 0), (0, width - w.shape[1])))


def _layer_params(l, norm_mix_pre, norm_mix_post, w_in, rwkv_mu, rwkv_w0, rwkv_w2, rwkv_a0, rwkv_a2,
                  rwkv_g2, rwkv_k_k, rwkv_k_a, rwkv_r_k, rwkv_ln_w, rwkv_ln_b, ssm_conv_w, ssm_conv_b,
                  ssm_dt_bias, ssm_A_log, ssm_D, ssm_norm_w, fox_f_bias, fox_norm_w, w_out,
                  norm_mlp_pre, norm_mlp_post, w_mlp_up, w_mlp_down):
    row = lambda x: x.reshape(1, -1).astype(F32)
    ssm_cols = C_M + CONV_DIM + H_M
    w = w_in[l]
    w_r = w[:, :RWKV_COLS]
    w_m = _pad_cols(w[:, RWKV_COLS:RWKV_COLS + ssm_cols], SSM_COLS_PAD)
    w_f = _pad_cols(w[:, RWKV_COLS + ssm_cols:], FOX_COLS_PAD)
    lo_rows = R_DECAY + R_AAA + R_GATE
    place = lambda m, r0: jnp.pad(m, ((r0, lo_rows - r0 - m.shape[0]), (0, 0)))
    inproj = dict(g=row(norm_mix_pre[l]), w_r=w_r.astype(BF16), w_m=w_m.astype(BF16), w_f=w_f.astype(BF16))
    rwkv = dict(mu=row(rwkv_mu[l]), w0=row(rwkv_w0[l]), w2=place(rwkv_w2[l], 0), a0=row(rwkv_a0[l]),
                a2=place(rwkv_a2[l], R_DECAY), g2=place(rwkv_g2[l], R_DECAY + R_AAA),
                k_k=row(rwkv_k_k[l]), k_a=row(rwkv_k_a[l]), r_k=row(rwkv_r_k[l]),
                ln_w=row(rwkv_ln_w[l]), ln_b=row(rwkv_ln_b[l]))
    ssd = dict(conv_w=ssm_conv_w[l], conv_b=row(ssm_conv_b[l]),
               dt_bias=_pad_cols(row(ssm_dt_bias[l]), LANE),
               a_neg=_pad_cols(row(-jnp.exp(ssm_A_log[l].astype(F32))), LANE),
               d_x=row(jnp.repeat(ssm_D[l], HEAD_DIM)), norm_w=row(ssm_norm_w[l]))
    fox = dict(f_bias=_pad_cols(row(fox_f_bias[l]), LANE), norm_w=row(fox_norm_w[l]))
    wo = w_out[l].astype(BF16)
    mlp = dict(wo_r=wo[:C_R], wo_m=wo[C_R:C_R + C_M], wo_f=wo[C_R + C_M:],
               g_mix_post=row(norm_mix_post[l]), g_mlp_pre=row(norm_mlp_pre[l]),
               g_mlp_post=row(norm_mlp_post[l]), w_up=w_mlp_up[l].astype(BF16),
               w_down=w_mlp_down[l].astype(BF16))
    return inproj, rwkv, ssd, fox, mlp


def kernel(x, norm_mix_pre, norm_mix_post, w_in, rwkv_mu, rwkv_w0, rwkv_w2, rwkv_a0, rwkv_a2, rwkv_g2, rwkv_k_k, rwkv_k_a, rwkv_r_k, rwkv_ln_w, rwkv_ln_b, ssm_conv_w, ssm_conv_b, ssm_dt_bias, ssm_A_log, ssm_D, ssm_norm_w, fox_f_bias, fox_norm_w, w_out, norm_mlp_pre, norm_mlp_post, w_mlp_up, w_mlp_down):
    weights = (norm_mix_pre, norm_mix_post, w_in, rwkv_mu, rwkv_w0, rwkv_w2, rwkv_a0, rwkv_a2, rwkv_g2,
               rwkv_k_k, rwkv_k_a, rwkv_r_k, rwkv_ln_w, rwkv_ln_b, ssm_conv_w, ssm_conv_b, ssm_dt_bias,
               ssm_A_log, ssm_D, ssm_norm_w, fox_f_bias, fox_norm_w, w_out, norm_mlp_pre, norm_mlp_post,
               w_mlp_up, w_mlp_down)
    bsz, t, d = x.shape
    n = bsz * t
    assert t % FOX_BLOCK == 0 and t % SSD_CHUNK == 0 and t % RWKV_CHUNK == 0
    tm = 512 if n % 512 == 0 else 256
    h = x.reshape(n, d)
    for l in range(w_in.shape[0]):
        inproj, rwkv, ssd, fox, mlp = _layer_params(l, *weights)
        p_r, p_m, p_f = _inproj(h, inproj["g"], inproj["w_r"], inproj["w_m"], inproj["w_f"], tm)
        y_r = _rwkv(p_r.reshape(bsz, t, -1), rwkv)
        y_m = _ssd(p_m.reshape(bsz, t, -1), ssd)
        p_f3 = p_f.reshape(bsz, t, -1)
        nck = _fox_prep(p_f3, fox["f_bias"])
        y_f = _fox(p_f3, nck, fox["norm_w"])
        h = _mix_mlp(h, y_r.reshape(n, -1), y_m.reshape(n, -1), y_f.reshape(n, -1), mlp, tm)
    return h.reshape(bsz, t, d)
```

```python
import functools
import math

import jax
import jax.numpy as jnp
from jax import lax
from jax.experimental import pallas as pl
from jax.experimental.pallas import tpu as pltpu

F32 = jnp.float32
BF16 = jnp.bfloat16

HEAD_DIM = 64
NORM_EPS = 1e-6
C_R = 256
R_DECAY, R_AAA, R_GATE = 32, 32, 64
RWKV_GN_EPS = 64e-5
RWKV_COLS = 3 * C_R + R_DECAY + R_AAA + R_GATE
RWKV_CHUNK = 64
RWKV_ROWS = 8
C_M = 512
H_M = 8
SSM_STATE = 128
SSM_GROUPS = 2
CONV_K = 4
CONV_DIM = C_M + 2 * SSM_GROUPS * SSM_STATE
SSD_CHUNK = 128
SSM_NORM_EPS = 1e-5
SSM_COLS_PAD = C_M + CONV_DIM + 128
C_F = 256
H_F = 4
FOX_COLS_PAD = 3 * C_F + 128
FOX_BLOCK = 512
FOX_STRIP = 32
LOG2E = 1.4426950408889634
LANE = 128
NEG_BIG = -1e30

VMEM_LIMIT = 48 * 1024 * 1024

_NN = (((1,), (0,)), ((), ()))
_NT = (((1,), (1,)), ((), ()))


def _split(x, n):
    if x.dtype == BF16:
        return [x]
    terms = []
    rem = x
    for i in range(n):
        t = rem.astype(BF16)
        terms.append(t)
        if i + 1 < n:
            rem = rem - t.astype(F32)
    return terms


def _mm(a, b, na=1, nb=1, dims=_NN):
    ta, tb = _split(a, na), _split(b, nb)
    depth = max(len(ta), len(tb))
    out = None
    for i, x in enumerate(ta):
        for j, y in enumerate(tb):
            if i + j < depth:
                d = lax.dot_general(x, y, dims, preferred_element_type=F32)
                out = d if out is None else out + d
    return out


def _sigmoid(x):
    return 1.0 / (1.0 + jnp.exp(-x))


def _softplus(x):
    return jnp.maximum(x, 0.0) + jnp.log1p(jnp.exp(-jnp.abs(x)))


def _iota(shape, dim):
    return lax.broadcasted_iota(jnp.int32, shape, dim)


def _block_diag(x, nblk, rblk, cblk):
    t = jnp.concatenate([x.astype(BF16)] * nblk, axis=0)
    shape = t.shape
    keep = (_iota(shape, 0) // rblk) == (_iota(shape, 1) // cblk)
    return jnp.where(keep, t, jnp.zeros((), t.dtype))


def _inproj_kernel(h_ref, g_ref, w_ref, pr_ref, pm_ref, pf_ref):
    x = h_ref[...]
    ms = jnp.mean(x * x, axis=-1, keepdims=True)
    u = (x * lax.rsqrt(ms + NORM_EPS) * g_ref[...]).astype(BF16)
    c0 = 0
    for o_ref in (pr_ref, pm_ref, pf_ref):
        c1 = c0 + o_ref.shape[1]
        o_ref[...] = jnp.dot(u, w_ref[:, c0:c1], preferred_element_type=F32)
        c0 = c1


def _const_spec(shape):
    nd = len(shape)
    return pl.BlockSpec(shape, lambda *_: (0,) * nd, pipeline_mode=pl.Buffered(1))


def _inproj_weights(w_in):
    ssm_end = RWKV_COLS + C_M + CONV_DIM + H_M
    cols = w_in.shape[-1]
    width = RWKV_COLS + SSM_COLS_PAD + FOX_COLS_PAD
    src = jnp.arange(cols, dtype=jnp.int32)
    dst = jnp.where(src < ssm_end, src, src - ssm_end + RWKV_COLS + SSM_COLS_PAD)
    place = (dst[:, None] == jnp.arange(width, dtype=jnp.int32)[None, :]).astype(BF16)
    return jnp.einsum("lkc,cd->lkd", w_in.astype(BF16), place, preferred_element_type=BF16)


def _inproj(h2d, g, w, tm):
    n, d = h2d.shape
    widths = (RWKV_COLS, SSM_COLS_PAD, FOX_COLS_PAD)
    assert w.shape[1] == sum(widths)
    return pl.pallas_call(
        _inproj_kernel,
        grid=(n // tm,),
        in_specs=[pl.BlockSpec((tm, d), lambda i: (i, 0)), _const_spec(g.shape), _const_spec(w.shape)],
        out_specs=[pl.BlockSpec((tm, c), lambda i: (i, 0)) for c in widths],
        out_shape=[jax.ShapeDtypeStruct((n, c), F32) for c in widths],
        name="inproj",
        compiler_params=pltpu.CompilerParams(
            dimension_semantics=("parallel",), vmem_limit_bytes=VMEM_LIMIT),
    )(h2d, g, w)


def _rwkv_kernel(p_ref, mu_ref, w0_ref, w2_ref, a0_ref, a2_ref, g2_ref, kk_ref, ka_ref,
                 rk_ref, lnw_ref, lnb_ref, o_ref, s_ref, prev_ref):
    nb, L, C = p_ref.shape[0], RWKV_CHUNK, C_R
    nh = C // HEAD_DIM
    R = nb * L

    @pl.when(pl.program_id(1) == 0)
    def _():
        s_ref[...] = jnp.zeros_like(s_ref)
        prev_ref[...] = jnp.zeros_like(prev_ref)

    p = jnp.concatenate([p_ref[bi] for bi in range(nb)], axis=0)
    row = _iota((R, 1), 0)
    shifted = pltpu.roll(p, 1, 0)
    for bi in range(nb):
        shifted = jnp.where(row == bi * L, prev_ref[bi, 7:8, :], shifted)
        prev_ref[bi] = p_ref[bi, L - 8:L, :]
    xs = p + (shifted - p) * mu_ref[...]
    r, k, v, lo = xs[:, :C], xs[:, C:2 * C], xs[:, 2 * C:3 * C], xs[:, 3 * C:]

    seg = ((_iota((C, C), 0) // HEAD_DIM) == (_iota((C, C), 1) // HEAD_DIM))
    seg_ones = jnp.where(seg, 1.0, 0.0).astype(BF16)
    seg_mean = jnp.where(seg, 1.0 / HEAD_DIM, 0.0).astype(BF16)

    logw = w0_ref[...] + _mm(jnp.tanh(lo), w2_ref[...], 2, 2)
    lw = -math.exp(-0.5) * _sigmoid(logw)
    lr = _sigmoid(a0_ref[...] + _mm(lo, a2_ref[...]))
    gate = _mm(_sigmoid(lo), g2_ref[...])
    kk = k * kk_ref[...]
    kk = kk / jnp.maximum(jnp.sqrt(_mm(kk * kk, seg_ones)), 1e-12)
    k = k * (1.0 + (lr - 1.0) * ka_ref[...])
    a = -kk
    b = kk * lr

    rr, cc = _iota((R, R), 0), _iota((R, R), 1)
    tri = jnp.where((rr >= cc) & (rr // L == cc // L), 1.0, 0.0).astype(BF16)
    cum = _mm(tri, lw, 1, 3)
    cl = jnp.concatenate([jnp.broadcast_to(cum[(bi + 1) * L - 1:(bi + 1) * L, :], (L, C))
                          for bi in range(nb)], axis=0)
    e_neg = jnp.exp(-cum)
    e_end = jnp.exp(cl - cum)
    chunks = lambda x: [x[bi * L:(bi + 1) * L] for bi in range(nb)]
    At, Rt, Kt, Bt = chunks(a * jnp.exp(cum - lw)), chunks(r * jnp.exp(cum)), chunks(k * e_neg), chunks(b * e_neg)
    Bp, Kp, vs = chunks(b * e_end), chunks(k * e_end), chunks(v)
    e_tot = [x[L - 1:L, :] for x in chunks(jnp.exp(cl))]

    bd = functools.partial(_block_diag, nblk=nh, rblk=L, cblk=HEAD_DIM)
    rows = range(nb)
    m1 = [_mm(jnp.concatenate([At[i], Rt[i]], axis=0),
              jnp.concatenate([bd(Bt[i]), bd(Kt[i])], axis=0), dims=_NT) for i in rows]
    tcol = _iota((L, C), 1) % HEAD_DIM
    trow = _iota((L, C), 0)
    strict = tcol < trow
    incl = tcol <= trow
    a_ak = [jnp.where(strict, m[:L, C:], 0.0) for m in m1]
    r_rb = [jnp.where(incl, m[L:, :C], 0.0) for m in m1]
    r_rk = [jnp.where(incl, m[L:, C:], 0.0) for m in m1]

    x = [jnp.where(strict, m[:L, :C], 0.0) for m in m1]
    eye = jnp.where(tcol == trow, 1.0, 0.0)
    t_inv = [eye + xi for xi in x]
    bx = [bd(xi) for xi in x]
    for _ in range(int(math.log2(L)) - 1):
        x = [_mm(xi, bi) for xi, bi in zip(x, bx)]
        bx = [bd(xi) for xi in x]
        t_inv = [ti + _mm(ti, bi) for ti, bi in zip(t_inv, bx)]

    s0 = [s_ref[i] for i in rows]
    bdv = [bd(vi) for vi in vs]
    rhs = [_mm(At[i], s0[i], dims=_NT) + _mm(a_ak[i], bdv[i]) for i in rows]
    u = [_mm(t_inv[i], bd(rhs[i])) for i in rows]
    y = [_mm(Rt[i], s0[i], dims=_NT) + _mm(r_rb[i], bd(u[i])) + _mm(r_rk[i], bdv[i])
         for i in rows]
    upd = [_mm(jnp.concatenate([u[i], vs[i]], axis=0).T,
               jnp.concatenate([Bp[i], Kp[i]], axis=0)) for i in rows]
    for i in rows:
        s_ref[i] = s0[i] * e_tot[i] + jnp.where(seg, upd[i], 0.0)

    y = jnp.concatenate(y, axis=0)
    mean = _mm(y, seg_mean)
    d = y - mean
    var = _mm(d * d, seg_mean)
    yn = d * lax.rsqrt(var + RWKV_GN_EPS) * lnw_ref[...] + lnb_ref[...]
    bonus = _mm(r * k * rk_ref[...], seg_ones) * v
    out = (yn + bonus) * gate
    for i in rows:
        o_ref[i] = out[i * L:(i + 1) * L]


def _rwkv(p_r, prm):
    bsz, t, cols = p_r.shape
    L = RWKV_CHUNK
    nb = RWKV_ROWS if bsz % RWKV_ROWS == 0 else 1
    names = ("mu", "w0", "w2", "a0", "a2", "g2", "k_k", "k_a", "r_k", "ln_w", "ln_b")
    params = [prm[nm] for nm in names]
    return pl.pallas_call(
        _rwkv_kernel,
        grid=(bsz // nb, t // L),
        in_specs=[pl.BlockSpec((nb, L, cols), lambda b, c: (b, c, 0))]
                 + [_const_spec(x.shape) for x in params],
        out_specs=pl.BlockSpec((nb, L, C_R), lambda b, c: (b, c, 0)),
        out_shape=jax.ShapeDtypeStruct((bsz, t, C_R), F32),
        scratch_shapes=[pltpu.VMEM((nb, C_R, C_R), F32), pltpu.VMEM((nb, 8, cols), F32)],
        name="rwkv7",
        compiler_params=pltpu.CompilerParams(
            dimension_semantics=("parallel", "arbitrary"), vmem_limit_bytes=VMEM_LIMIT),
    )(p_r, *params)


def _ssd_kernel(p_ref, cw_ref, cb_ref, dtb_ref, a_ref, d_ref, nw_ref, o_ref, xe_ref, s_ref):
    L, N = SSD_CHUNK, SSM_STATE
    hg = H_M // SSM_GROUPS
    gw = hg * HEAD_DIM

    @pl.when(pl.program_id(1) == 0)
    def _():
        s_ref[...] = jnp.zeros_like(s_ref)
        xe_ref[0:8, :] = jnp.zeros((8, CONV_DIM), F32)

    @pl.when(pl.program_id(1) > 0)
    def _():
        xe_ref[0:8, :] = xe_ref[L:L + 8, :]

    p = p_ref[...]
    z = p[:, :C_M]
    xbc = p[:, C_M:C_M + CONV_DIM]
    xe_ref[8:8 + L, :] = xbc
    cw = cw_ref[...]
    conv = cb_ref[...] + xbc * cw[CONV_K - 1:CONV_K, :]
    for s in range(1, CONV_K):
        conv = conv + xe_ref[8 - s:8 - s + L, :] * cw[CONV_K - 1 - s:CONV_K - s, :]
    act = conv * _sigmoid(conv)
    xs = act[:, :C_M]
    bm = act[:, C_M:C_M + SSM_GROUPS * N]
    cm = act[:, C_M + SSM_GROUPS * N:]

    dt = _softplus(p[:, C_M + CONV_DIM:] + dtb_ref[...])
    a = dt * a_ref[...]
    tri = jnp.where(_iota((L, L), 0) >= _iota((L, L), 1), 1.0, 0.0).astype(BF16)
    acum = _mm(tri, a, 1, 3)
    acum_cols = [jnp.broadcast_to(acum[:, h:h + 1], (L, LANE)) for h in range(H_M)]
    dt_cols = [jnp.broadcast_to(dt[:, h:h + 1], (L, LANE)) for h in range(H_M)]
    first = _iota((L, LANE), 1) < HEAD_DIM
    pairs = lambda cols: jnp.concatenate(
        [jnp.where(first, cols[h], cols[h + 1]) for h in range(0, H_M, 2)], axis=1)
    acum_x = pairs(acum_cols)
    dt_x = pairs(dt_cols)
    acum_row = acum.T
    atot_x = acum_x[L - 1:L, :]
    xdt = xs * dt_x
    x_end = xdt * jnp.exp(atot_x - acum_x)
    e_in = jnp.exp(acum_x)
    e_tot = jnp.exp(atot_x)
    causal = _iota((L, L), 0) >= _iota((L, L), 1)

    s_old = s_ref[...]
    ys = []
    for g in range(SSM_GROUPS):
        bg = bm[:, g * N:(g + 1) * N]
        cg = cm[:, g * N:(g + 1) * N]
        cb = _mm(cg, bg, 1, 1, _NT)
        lmats = []
        for e in range(hg):
            h = g * hg + e
            diff = acum_cols[h] - acum_row[h:h + 1, :]
            lmats.append(cb * jnp.exp(jnp.where(causal, diff, NEG_BIG)))
        gs = slice(g * gw, (g + 1) * gw)
        y_diag = _mm(jnp.concatenate(lmats, axis=1), _block_diag(xdt[:, gs], hg, L, HEAD_DIM))
        sg = s_old[:, gs]
        y_off = _mm(cg, sg) * e_in[:, gs]
        s_ref[:, gs] = sg * e_tot[:, gs] + _mm(bg.T, x_end[:, gs])
        ys.append(y_diag + y_off)
    y = jnp.concatenate(ys, axis=1) + d_ref[...] * xs
    y = y * (z * _sigmoid(z))
    outs = []
    for g in range(SSM_GROUPS):
        yg = y[:, g * gw:(g + 1) * gw]
        outs.append(yg * lax.rsqrt(jnp.mean(yg * yg, axis=-1, keepdims=True) + SSM_NORM_EPS))
    o_ref[...] = jnp.concatenate(outs, axis=1) * nw_ref[...]


def _ssd(p_m, prm):
    bsz, t, cols = p_m.shape
    L = SSD_CHUNK
    names = ("conv_w", "conv_b", "dt_bias", "a_neg", "d_x", "norm_w")
    params = [prm[nm] for nm in names]
    return pl.pallas_call(
        _ssd_kernel,
        grid=(bsz, t // L),
        in_specs=[pl.BlockSpec((None, L, cols), lambda b, c: (b, c, 0))]
                 + [_const_spec(x.shape) for x in params],
        out_specs=pl.BlockSpec((None, L, C_M), lambda b, c: (b, c, 0)),
        out_shape=jax.ShapeDtypeStruct((bsz, t, C_M), F32),
        scratch_shapes=[pltpu.VMEM((L + 8, CONV_DIM), F32), pltpu.VMEM((SSM_STATE, C_M), F32)],
        name="ssd",
        compiler_params=pltpu.CompilerParams(
            dimension_semantics=("parallel", "arbitrary"), vmem_limit_bytes=VMEM_LIMIT),
    )(p_m, *params)


def _fox_prep_kernel(f_ref, fb_ref, nck_ref, carry_ref):
    L = f_ref.shape[0]

    @pl.when(pl.program_id(1) == 0)
    def _():
        carry_ref[...] = jnp.zeros_like(carry_ref)

    log_f = -_softplus(-(f_ref[...] + fb_ref[...]))
    tri = jnp.where(_iota((L, L), 0) >= _iota((L, L), 1), 1.0, 0.0).astype(BF16)
    c = _mm(tri, log_f, 1, 3) + carry_ref[0:1, :]
    carry_ref[...] = jnp.broadcast_to(c[L - 1:L, :], carry_ref.shape)
    nck_ref[...] = (c * (-LOG2E)).T[0:8, :]


def _fox_prep(p_f, f_bias):
    bsz, t, cols = p_f.shape
    L = FOX_BLOCK
    fblk = (cols - LANE) // LANE
    return pl.pallas_call(
        _fox_prep_kernel,
        grid=(bsz, t // L),
        in_specs=[pl.BlockSpec((None, L, LANE), lambda b, c: (b, c, fblk)), _const_spec(f_bias.shape)],
        out_specs=pl.BlockSpec((None, 8, L), lambda b, c: (b, 0, c)),
        out_shape=jax.ShapeDtypeStruct((bsz, 8, t), F32),
        scratch_shapes=[pltpu.VMEM((8, LANE), F32)],
        name="fox_prep",
        compiler_params=pltpu.CompilerParams(
            dimension_semantics=("parallel", "arbitrary"), vmem_limit_bytes=VMEM_LIMIT),
    )(p_f, f_bias)


def _fox_kernel(q_ref, k_ref, v_ref, nck_ref, nw_ref, o_ref, q2_ref, m_ref, l_ref, al_ref, p_ref,
                acc_ref):
    i, j = pl.program_id(1), pl.program_id(2)
    tq, tk = q_ref.shape[0], k_ref.shape[0]
    C = C_F
    zero = jnp.zeros((), BF16)

    def per_head_lanes(xs):
        first = _iota((tq, LANE), 1) < HEAD_DIM
        return jnp.concatenate([jnp.where(first, xs[0], xs[1]), jnp.where(first, xs[2], xs[3])], axis=1)

    @pl.when(j == 0)
    def _():
        q2 = (q_ref[...] * (LOG2E / math.sqrt(HEAD_DIM))).astype(BF16)
        head = _iota((tq, C), 1) // HEAD_DIM
        for h in range(H_F):
            q2_ref[h] = jnp.where(head == h, q2, zero)
        m_ref[...] = jnp.full(m_ref.shape, NEG_BIG, F32)
        l_ref[...] = jnp.zeros_like(l_ref)
        acc_ref[...] = jnp.zeros_like(acc_ref)

    def step(masked):
        kb = k_ref[...].astype(BF16)
        vb = v_ref[...].astype(BF16)
        head = _iota((tk, C), 1) // HEAD_DIM
        nck = nck_ref[...]
        rs = FOX_STRIP
        scores = lambda h: lax.dot_general(q2_ref[h], kb, _NT, preferred_element_type=F32)
        s_next = scores(0)
        for h in range(H_F):
            s_all = s_next
            if h + 1 < H_F:
                s_next = scores(h + 1)
            nk = nck[h:h + 1, :]
            for r in range(0, tq, rs):
                sh = s_all[r:r + rs, :] + nk
                if masked:
                    sh = jnp.where(r + _iota((rs, tk), 0) >= _iota((rs, tk), 1), sh, NEG_BIG)
                m_old = m_ref[h, r:r + rs, :]
                m_new = jnp.maximum(m_old, jnp.max(sh, axis=-1, keepdims=True))
                m_ref[h, r:r + rs, :] = m_new
                alpha = jnp.exp2(m_old - m_new)
                p = jnp.exp2(sh - jnp.concatenate([m_new] * (tk // LANE), axis=1))
                l_ref[h, r:r + rs, :] = alpha * l_ref[h, r:r + rs, :] + jnp.sum(p, axis=-1, keepdims=True)
                al_ref[h, r:r + rs, :] = alpha
                p_ref[r:r + rs, h * tk:(h + 1) * tk] = p.astype(BF16)
        vstack = jnp.concatenate([jnp.where(head == h, vb, zero) for h in range(H_F)], axis=0)
        acc_ref[...] = acc_ref[...] * per_head_lanes([al_ref[h] for h in range(H_F)]) + jnp.dot(
            p_ref[...], vstack, preferred_element_type=F32)

    @pl.when(j < i)
    def _():
        step(False)

    @pl.when(j == i)
    def _():
        step(True)
        y = acc_ref[...] / per_head_lanes([l_ref[h] for h in range(H_F)])
        seg = ((_iota((C, C), 0) // HEAD_DIM) == (_iota((C, C), 1) // HEAD_DIM))
        ms = _mm(y * y, jnp.where(seg, 1.0 / HEAD_DIM, 0.0).astype(BF16), 3, 1)
        o_ref[...] = y * lax.rsqrt(ms + NORM_EPS) * nw_ref[...]


def _fox(p_f, nck, norm_w):
    bsz, t, _ = p_f.shape
    blk = FOX_BLOCK
    nb = t // blk
    kv = lambda col: (lambda b, i, j: (b, jnp.minimum(j, i), col))
    return pl.pallas_call(
        _fox_kernel,
        grid=(bsz, nb, nb),
        in_specs=[pl.BlockSpec((None, blk, C_F), lambda b, i, j: (b, i, 0)),
                  pl.BlockSpec((None, blk, C_F), kv(1)),
                  pl.BlockSpec((None, blk, C_F), kv(2)),
                  pl.BlockSpec((None, 8, blk), lambda b, i, j: (b, 0, jnp.minimum(j, i))),
                  _const_spec(norm_w.shape)],
        out_specs=pl.BlockSpec((None, blk, C_F), lambda b, i, j: (b, i, 0)),
        out_shape=jax.ShapeDtypeStruct((bsz, t, C_F), F32),
        scratch_shapes=[pltpu.VMEM((H_F, blk, C_F), BF16), pltpu.VMEM((H_F, blk, LANE), F32),
                        pltpu.VMEM((H_F, blk, LANE), F32), pltpu.VMEM((H_F, blk, LANE), F32),
                        pltpu.VMEM((blk, H_F * blk), BF16), pltpu.VMEM((blk, C_F), F32)],
        name="fox_attn",
        compiler_params=pltpu.CompilerParams(
            dimension_semantics=("parallel", "parallel", "arbitrary"), vmem_limit_bytes=VMEM_LIMIT),
    )(p_f, p_f, p_f, nck, norm_w)


def _rms(x, g):
    return x * lax.rsqrt(jnp.mean(x * x, axis=-1, keepdims=True) + NORM_EPS) * g


def _mix_mlp_kernel(h_ref, yr_ref, ym_ref, yf_ref, wor_ref, wom_ref, wof_ref, g1_ref, g2_ref, g3_ref,
                    wu_ref, wd_ref, o_ref):
    mix = (jnp.dot(yr_ref[...].astype(BF16), wor_ref[...], preferred_element_type=F32)
           + jnp.dot(ym_ref[...].astype(BF16), wom_ref[...], preferred_element_type=F32)
           + jnp.dot(yf_ref[...].astype(BF16), wof_ref[...], preferred_element_type=F32))
    h1 = h_ref[...] + _rms(mix, g1_ref[...])
    u = _rms(h1, g2_ref[...]).astype(BF16)
    d_ff = wu_ref.shape[1]
    step = 1024
    ff = None
    for c0 in range(0, d_ff, step):
        hid = jnp.dot(u, wu_ref[:, c0:c0 + step], preferred_element_type=F32)
        hid = jnp.square(jnp.maximum(hid, 0.0)).astype(BF16)
        part = jnp.dot(hid, wd_ref[c0:c0 + step, :], preferred_element_type=F32)
        ff = part if ff is None else ff + part
    o_ref[...] = h1 + _rms(ff, g3_ref[...])


def _mix_mlp(h2d, yr, ym, yf, prm, tm):
    n, d = h2d.shape
    names = ("wo_r", "wo_m", "wo_f", "g_mix_post", "g_mlp_pre", "g_mlp_post", "w_up", "w_down")
    params = [prm[nm] for nm in names]
    row = lambda w: pl.BlockSpec((tm, w), lambda i: (i, 0))
    return pl.pallas_call(
        _mix_mlp_kernel,
        grid=(n // tm,),
        in_specs=[row(d), row(yr.shape[1]), row(ym.shape[1]), row(yf.shape[1])]
                 + [_const_spec(x.shape) for x in params],
        out_specs=row(d),
        out_shape=jax.ShapeDtypeStruct((n, d), F32),
        name="mix_mlp",
        compiler_params=pltpu.CompilerParams(
            dimension_semantics=("parallel",), vmem_limit_bytes=VMEM_LIMIT),
    )(h2d, yr, ym, yf, *params)


def _pad_cols(w, width):
    return jnp.pad(w, ((0, 0), (0, width - w.shape[1])))


def _layer_params(l, norm_mix_pre, norm_mix_post, w_in, rwkv_mu, rwkv_w0, rwkv_w2, rwkv_a0, rwkv_a2,
                  rwkv_g2, rwkv_k_k, rwkv_k_a, rwkv_r_k, rwkv_ln_w, rwkv_ln_b, ssm_conv_w, ssm_conv_b,
                  ssm_dt_bias, ssm_A_log, ssm_D, ssm_norm_w, fox_f_bias, fox_norm_w, w_out,
                  norm_mlp_pre, norm_mlp_post, w_mlp_up, w_mlp_down):
    row = lambda x: x.reshape(1, -1).astype(F32)
    lo_rows = R_DECAY + R_AAA + R_GATE
    place = lambda m, r0: jnp.pad(m, ((r0, lo_rows - r0 - m.shape[0]), (0, 0)))
    inproj = dict(g=row(norm_mix_pre[l]))
    rwkv = dict(mu=row(rwkv_mu[l]), w0=row(rwkv_w0[l]), w2=place(rwkv_w2[l], 0), a0=row(rwkv_a0[l]),
                a2=place(rwkv_a2[l], R_DECAY), g2=place(rwkv_g2[l], R_DECAY + R_AAA),
                k_k=row(rwkv_k_k[l]), k_a=row(rwkv_k_a[l]), r_k=row(rwkv_r_k[l]),
                ln_w=row(rwkv_ln_w[l]), ln_b=row(rwkv_ln_b[l]))
    ssd = dict(conv_w=ssm_conv_w[l], conv_b=row(ssm_conv_b[l]),
               dt_bias=_pad_cols(row(ssm_dt_bias[l]), LANE),
               a_neg=_pad_cols(row(-jnp.exp(ssm_A_log[l].astype(F32))), LANE),
               d_x=row(jnp.repeat(ssm_D[l], HEAD_DIM)), norm_w=row(ssm_norm_w[l]))
    fox = dict(f_bias=_pad_cols(row(fox_f_bias[l]), LANE), norm_w=row(fox_norm_w[l]))
    wo = w_out[l].astype(BF16)
    mlp = dict(wo_r=wo[:C_R], wo_m=wo[C_R:C_R + C_M], wo_f=wo[C_R + C_M:],
               g_mix_post=row(norm_mix_post[l]), g_mlp_pre=row(norm_mlp_pre[l]),
               g_mlp_post=row(norm_mlp_post[l]), w_up=w_mlp_up[l].astype(BF16),
               w_down=w_mlp_down[l].astype(BF16))
    return inproj, rwkv, ssd, fox, mlp


def kernel(x, norm_mix_pre, norm_mix_post, w_in, rwkv_mu, rwkv_w0, rwkv_w2, rwkv_a0, rwkv_a2, rwkv_g2, rwkv_k_k, rwkv_k_a, rwkv_r_k, rwkv_ln_w, rwkv_ln_b, ssm_conv_w, ssm_conv_b, ssm_dt_bias, ssm_A_log, ssm_D, ssm_norm_w, fox_f_bias, fox_norm_w, w_out, norm_mlp_pre, norm_mlp_post, w_mlp_up, w_mlp_down):
    weights = (norm_mix_pre, norm_mix_post, w_in, rwkv_mu, rwkv_w0, rwkv_w2, rwkv_a0, rwkv_a2, rwkv_g2,
               rwkv_k_k, rwkv_k_a, rwkv_r_k, rwkv_ln_w, rwkv_ln_b, ssm_conv_w, ssm_conv_b, ssm_dt_bias,
               ssm_A_log, ssm_D, ssm_norm_w, fox_f_bias, fox_norm_w, w_out, norm_mlp_pre, norm_mlp_post,
               w_mlp_up, w_mlp_down)
    bsz, t, d = x.shape
    n = bsz * t
    assert t % FOX_BLOCK == 0 and t % SSD_CHUNK == 0 and t % RWKV_CHUNK == 0
    tm = 512 if n % 512 == 0 else 256
    h = x.reshape(n, d)
    w_proj = _inproj_weights(w_in)
    for l in range(w_in.shape[0]):
        inproj, rwkv, ssd, fox, mlp = _layer_params(l, *weights)
        p_r, p_m, p_f = _inproj(h, inproj["g"], w_proj[l], tm)
        y_r = _rwkv(p_r.reshape(bsz, t, -1), rwkv)
        y_m = _ssd(p_m.reshape(bsz, t, -1), ssd)
        p_f3 = p_f.reshape(bsz, t, -1)
        nck = _fox_prep(p_f3, fox["f_bias"])
        y_f = _fox(p_f3, nck, fox["norm_w"])
        h = _mix_mlp(h, y_r.reshape(n, -1), y_m.reshape(n, -1), y_f.reshape(n, -1), mlp, tm)
    return h.reshape(bsz, t, d)
```

```python
import functools
import itertools
import math

import jax
import jax.numpy as jnp
from jax import lax
from jax.experimental import pallas as pl
from jax.experimental.pallas import tpu as pltpu

F32 = jnp.float32
BF16 = jnp.bfloat16

HEAD_DIM = 64
NORM_EPS = 1e-6
C_R = 256
R_DECAY, R_AAA, R_GATE = 32, 32, 64
RWKV_GN_EPS = 64e-5
RWKV_COLS = 3 * C_R + R_DECAY + R_AAA + R_GATE
RWKV_CHUNK = 64
RWKV_ROWS = 8
C_M = 512
H_M = 8
SSM_STATE = 128
SSM_GROUPS = 2
CONV_K = 4
CONV_DIM = C_M + 2 * SSM_GROUPS * SSM_STATE
SSD_CHUNK = 128
SSD_ROWS = 4
SSM_NORM_EPS = 1e-5
SSM_COLS_PAD = C_M + CONV_DIM + 128
C_F = 256
H_F = 4
FOX_COLS_PAD = 3 * C_F + 128
FOX_BLOCK = 512
FOX_STRIP = 32
FOX_ROWS = 2
LOG2E = 1.4426950408889634
LANE = 128
NEG_BIG = -1e30

VMEM_LIMIT = 48 * 1024 * 1024

_NN = (((1,), (0,)), ((), ()))
_NT = (((1,), (1,)), ((), ()))


def _split(x, n):
    if x.dtype == BF16:
        return [x]
    terms = []
    rem = x
    for i in range(n):
        t = rem.astype(BF16)
        terms.append(t)
        if i + 1 < n:
            rem = rem - t.astype(F32)
    return terms


def _mm(a, b, na=1, nb=1, dims=_NN):
    ta, tb = _split(a, na), _split(b, nb)
    depth = max(len(ta), len(tb))
    out = None
    for i, x in enumerate(ta):
        for j, y in enumerate(tb):
            if i + j < depth:
                d = lax.dot_general(x, y, dims, preferred_element_type=F32)
                out = d if out is None else out + d
    return out


def _sigmoid(x):
    return 1.0 / (1.0 + jnp.exp(-x))


def _softplus(x):
    return jnp.maximum(x, 0.0) + jnp.log1p(jnp.exp(-jnp.abs(x)))


def _iota(shape, dim):
    return lax.broadcasted_iota(jnp.int32, shape, dim)


def _block_diag(x, nblk, rblk, cblk):
    t = jnp.concatenate([x.astype(BF16)] * nblk, axis=0)
    shape = t.shape
    keep = (_iota(shape, 0) // rblk) == (_iota(shape, 1) // cblk)
    return jnp.where(keep, t, jnp.zeros((), t.dtype))


def _inproj_kernel(h_ref, g_ref, w_ref, pr_ref, pm_ref, pf_ref):
    x = h_ref[...]
    ms = jnp.mean(x * x, axis=-1, keepdims=True)
    u = (x * lax.rsqrt(ms + NORM_EPS) * g_ref[...]).astype(BF16)
    proj = jnp.dot(u, w_ref[...], preferred_element_type=F32)
    c0 = 0
    for o_ref in (pr_ref, pm_ref, pf_ref):
        c1 = c0 + o_ref.shape[1]
        o_ref[...] = proj[:, c0:c1]
        c0 = c1


def _const_spec(shape):
    nd = len(shape)
    return pl.BlockSpec(shape, lambda *_: (0,) * nd, pipeline_mode=pl.Buffered(1))


def _inproj_weights(w_in):
    ssm_end = RWKV_COLS + C_M + CONV_DIM + H_M
    cols = w_in.shape[-1]
    width = RWKV_COLS + SSM_COLS_PAD + FOX_COLS_PAD
    src = jnp.arange(cols, dtype=jnp.int32)
    dst = jnp.where(src < ssm_end, src, src - ssm_end + RWKV_COLS + SSM_COLS_PAD)
    place = (dst[:, None] == jnp.arange(width, dtype=jnp.int32)[None, :]).astype(BF16)
    return jnp.einsum("lkc,cd->lkd", w_in.astype(BF16), place, preferred_element_type=BF16)


def _inproj(h2d, g, w, tm):
    n, d = h2d.shape
    widths = (RWKV_COLS, SSM_COLS_PAD, FOX_COLS_PAD)
    assert w.shape[1] == sum(widths)
    return pl.pallas_call(
        _inproj_kernel,
        grid=(n // tm,),
        in_specs=[pl.BlockSpec((tm, d), lambda i: (i, 0)), _const_spec(g.shape), _const_spec(w.shape)],
        out_specs=[pl.BlockSpec((tm, c), lambda i: (i, 0)) for c in widths],
        out_shape=[jax.ShapeDtypeStruct((n, c), F32) for c in widths],
        name="inproj",
        compiler_params=pltpu.CompilerParams(
            dimension_semantics=("parallel",), vmem_limit_bytes=VMEM_LIMIT),
    )(h2d, g, w)


def _rwkv_kernel(p_ref, mu_ref, w0_ref, w2_ref, a0_ref, a2_ref, g2_ref, kk_ref, ka_ref,
                 rk_ref, lnw_ref, lnb_ref, o_ref, s_ref, prev_ref):
    nb, L, C = p_ref.shape[0], RWKV_CHUNK, C_R
    nh = C // HEAD_DIM
    R = nb * L

    @pl.when(pl.program_id(1) == 0)
    def _():
        s_ref[...] = jnp.zeros_like(s_ref)
        prev_ref[...] = jnp.zeros_like(prev_ref)

    p = jnp.concatenate([p_ref[bi] for bi in range(nb)], axis=0)
    row = _iota((R, 1), 0)
    shifted = pltpu.roll(p, 1, 0)
    for bi in range(nb):
        shifted = jnp.where(row == bi * L, prev_ref[bi, 7:8, :], shifted)
        prev_ref[bi] = p_ref[bi, L - 8:L, :]
    xs = p + (shifted - p) * mu_ref[...]
    r, k, v, lo = xs[:, :C], xs[:, C:2 * C], xs[:, 2 * C:3 * C], xs[:, 3 * C:]

    seg = ((_iota((C, C), 0) // HEAD_DIM) == (_iota((C, C), 1) // HEAD_DIM))
    seg_ones = jnp.where(seg, 1.0, 0.0).astype(BF16)
    seg_mean = jnp.where(seg, 1.0 / HEAD_DIM, 0.0).astype(BF16)

    logw = w0_ref[...] + _mm(jnp.tanh(lo), w2_ref[...], 2, 2)
    lw = -math.exp(-0.5) * _sigmoid(logw)
    lr = _sigmoid(a0_ref[...] + _mm(lo, a2_ref[...]))
    gate = _mm(_sigmoid(lo), g2_ref[...])
    kk = k * kk_ref[...]
    kk = kk / jnp.maximum(jnp.sqrt(_mm(kk * kk, seg_ones)), 1e-12)
    k = k * (1.0 + (lr - 1.0) * ka_ref[...])
    a = -kk
    b = kk * lr

    rr, cc = _iota((R, R), 0), _iota((R, R), 1)
    tri = jnp.where((rr >= cc) & (rr // L == cc // L), 1.0, 0.0).astype(BF16)
    cum = _mm(tri, lw, 1, 3)
    cl = jnp.concatenate([jnp.broadcast_to(cum[(bi + 1) * L - 1:(bi + 1) * L, :], (L, C))
                          for bi in range(nb)], axis=0)
    e_neg = jnp.exp(-cum)
    e_end = jnp.exp(cl - cum)
    chunks = lambda x: [x[bi * L:(bi + 1) * L] for bi in range(nb)]
    At, Rt, Kt, Bt = chunks(a * jnp.exp(cum - lw)), chunks(r * jnp.exp(cum)), chunks(k * e_neg), chunks(b * e_neg)
    Bp, Kp, vs = chunks(b * e_end), chunks(k * e_end), chunks(v)
    e_tot = [x[L - 1:L, :] for x in chunks(jnp.exp(cl))]

    bd = functools.partial(_block_diag, nblk=nh, rblk=L, cblk=HEAD_DIM)
    rows = range(nb)
    lhs = [jnp.concatenate([At[i], Rt[i]], axis=0) for i in rows]
    m1 = [_mm(lhs[i], jnp.concatenate([bd(Bt[i]), bd(Kt[i])], axis=0), dims=_NT) for i in rows]
    tcol = _iota((L, C), 1) % HEAD_DIM
    trow = _iota((L, C), 0)
    strict = tcol < trow
    incl = tcol <= trow
    a_ak = [jnp.where(strict, m[:L, C:], 0.0) for m in m1]
    r_rb = [jnp.where(incl, m[L:, :C], 0.0) for m in m1]
    r_rk = [jnp.where(incl, m[L:, C:], 0.0) for m in m1]

    x = [jnp.where(strict, m[:L, :C], 0.0) for m in m1]
    eye = jnp.where(tcol == trow, 1.0, 0.0)
    t_inv = [eye + xi for xi in x]
    x = [_mm(xi, bd(xi)) for xi in x]
    for _ in range(int(math.log2(L)) - 2):
        both = [_mm(jnp.concatenate([ti, xi], axis=0), bd(xi)) for ti, xi in zip(t_inv, x)]
        t_inv = [ti + b[:L] for ti, b in zip(t_inv, both)]
        x = [b[L:] for b in both]
    t_inv = [ti + _mm(ti, bd(xi)) for ti, xi in zip(t_inv, x)]

    s0 = [s_ref[i] for i in rows]
    from_s = [_mm(lhs[i], s0[i], dims=_NT) for i in rows]
    from_v = [_mm(jnp.concatenate([a_ak[i], r_rk[i]], axis=0), bd(vs[i])) for i in rows]
    u = [_mm(t_inv[i], bd(from_s[i][:L] + from_v[i][:L])) for i in rows]
    y = [from_s[i][L:] + from_v[i][L:] + _mm(r_rb[i], bd(u[i])) for i in rows]
    upd = [_mm(jnp.concatenate([u[i], vs[i]], axis=0).T,
               jnp.concatenate([Bp[i], Kp[i]], axis=0)) for i in rows]
    for i in rows:
        s_ref[i] = s0[i] * e_tot[i] + jnp.where(seg, upd[i], 0.0)

    y = jnp.concatenate(y, axis=0)
    mean = _mm(y, seg_mean)
    d = y - mean
    var = _mm(d * d, seg_mean)
    yn = d * lax.rsqrt(var + RWKV_GN_EPS) * lnw_ref[...] + lnb_ref[...]
    bonus = _mm(r * k * rk_ref[...], seg_ones) * v
    out = (yn + bonus) * gate
    for i in rows:
        o_ref[i] = out[i * L:(i + 1) * L]


def _rwkv(p_r, prm):
    bsz, t, cols = p_r.shape
    L = RWKV_CHUNK
    nb = RWKV_ROWS if bsz % RWKV_ROWS == 0 else 1
    names = ("mu", "w0", "w2", "a0", "a2", "g2", "k_k", "k_a", "r_k", "ln_w", "ln_b")
    params = [prm[nm] for nm in names]
    return pl.pallas_call(
        _rwkv_kernel,
        grid=(bsz // nb, t // L),
        in_specs=[pl.BlockSpec((nb, L, cols), lambda b, c: (b, c, 0))]
                 + [_const_spec(x.shape) for x in params],
        out_specs=pl.BlockSpec((nb, L, C_R), lambda b, c: (b, c, 0)),
        out_shape=jax.ShapeDtypeStruct((bsz, t, C_R), F32),
        scratch_shapes=[pltpu.VMEM((nb, C_R, C_R), F32), pltpu.VMEM((nb, 8, cols), F32)],
        name="rwkv7",
        compiler_params=pltpu.CompilerParams(
            dimension_semantics=("parallel", "arbitrary"), vmem_limit_bytes=VMEM_LIMIT),
    )(p_r, *params)


def _ssd_kernel(p_ref, cw_ref, cb_ref, dtb_ref, a_ref, d_ref, nw_ref, o_ref, xe_ref, s_ref):
    rows = [_ssd_row(p_ref.at[b], cw_ref, cb_ref, dtb_ref, a_ref, d_ref, nw_ref, o_ref.at[b], xe_ref.at[b],
                     s_ref.at[b]) for b in range(p_ref.shape[0])]
    _lockstep(rows)


def _ssd_row(p_ref, cw_ref, cb_ref, dtb_ref, a_ref, d_ref, nw_ref, o_ref, xe_ref, s_ref):
    L, N = SSD_CHUNK, SSM_STATE
    hg = H_M // SSM_GROUPS
    gw = hg * HEAD_DIM

    @pl.when(pl.program_id(1) == 0)
    def _():
        s_ref[...] = jnp.zeros_like(s_ref)
        xe_ref[0:8, :] = jnp.zeros((8, CONV_DIM), F32)

    @pl.when(pl.program_id(1) > 0)
    def _():
        xe_ref[0:8, :] = xe_ref[L:L + 8, :]

    p = p_ref[...]
    z = p[:, :C_M]
    xbc = p[:, C_M:C_M + CONV_DIM]
    xe_ref[8:8 + L, :] = xbc
    cw = cw_ref[...]
    conv = cb_ref[...] + xbc * cw[CONV_K - 1:CONV_K, :]
    for s in range(1, CONV_K):
        conv = conv + xe_ref[8 - s:8 - s + L, :] * cw[CONV_K - 1 - s:CONV_K - s, :]
    act = conv * _sigmoid(conv)
    yield
    xs = act[:, :C_M]
    bm = act[:, C_M:C_M + SSM_GROUPS * N]
    cm = act[:, C_M + SSM_GROUPS * N:]

    dt = _softplus(p[:, C_M + CONV_DIM:] + dtb_ref[...])
    a = dt * a_ref[...]
    tri = jnp.where(_iota((L, L), 0) >= _iota((L, L), 1), 1.0, 0.0).astype(BF16)
    acum = _mm(tri, a, 1, 3)
    yield
    acum_cols = [jnp.broadcast_to(acum[:, h:h + 1], (L, LANE)) for h in range(H_M)]
    dt_cols = [jnp.broadcast_to(dt[:, h:h + 1], (L, LANE)) for h in range(H_M)]
    first = _iota((L, LANE), 1) < HEAD_DIM
    pairs = lambda cols: jnp.concatenate(
        [jnp.where(first, cols[h], cols[h + 1]) for h in range(0, H_M, 2)], axis=1)
    acum_x = pairs(acum_cols)
    dt_x = pairs(dt_cols)
    acum_row = acum.T
    atot_x = acum_x[L - 1:L, :]
    xdt = xs * dt_x
    x_end = xdt * jnp.exp(atot_x - acum_x)
    e_in = jnp.exp(acum_x)
    e_tot = jnp.exp(atot_x)
    causal = _iota((L, L), 0) >= _iota((L, L), 1)
    yield

    s_old = s_ref[...]
    ys = []
    for g in range(SSM_GROUPS):
        bg = bm[:, g * N:(g + 1) * N]
        cg = cm[:, g * N:(g + 1) * N]
        cb = _mm(cg, bg, 1, 1, _NT)
        yield
        lmats = []
        for e in range(hg):
            h = g * hg + e
            diff = acum_cols[h] - acum_row[h:h + 1, :]
            lmats.append(cb * jnp.exp(jnp.where(causal, diff, NEG_BIG)))
        gs = slice(g * gw, (g + 1) * gw)
        yield
        y_diag = _mm(jnp.concatenate(lmats, axis=1), _block_diag(xdt[:, gs], hg, L, HEAD_DIM))
        sg = s_old[:, gs]
        y_off = _mm(cg, sg) * e_in[:, gs]
        s_ref[:, gs] = sg * e_tot[:, gs] + _mm(bg.T, x_end[:, gs])
        ys.append(y_diag + y_off)
        yield
    y = jnp.concatenate(ys, axis=1) + d_ref[...] * xs
    y = y * (z * _sigmoid(z))
    yield
    outs = []
    for g in range(SSM_GROUPS):
        yg = y[:, g * gw:(g + 1) * gw]
        outs.append(yg * lax.rsqrt(jnp.mean(yg * yg, axis=-1, keepdims=True) + SSM_NORM_EPS))
    o_ref[...] = jnp.concatenate(outs, axis=1) * nw_ref[...]
    yield


def _ssd(p_m, prm):
    bsz, t, cols = p_m.shape
    L = SSD_CHUNK
    nb = SSD_ROWS if bsz % SSD_ROWS == 0 else 1
    names = ("conv_w", "conv_b", "dt_bias", "a_neg", "d_x", "norm_w")
    params = [prm[nm] for nm in names]
    return pl.pallas_call(
        _ssd_kernel,
        grid=(bsz // nb, t // L),
        in_specs=[pl.BlockSpec((nb, L, cols), lambda b, c: (b, c, 0))]
                 + [_const_spec(x.shape) for x in params],
        out_specs=pl.BlockSpec((nb, L, C_M), lambda b, c: (b, c, 0)),
        out_shape=jax.ShapeDtypeStruct((bsz, t, C_M), F32),
        scratch_shapes=[pltpu.VMEM((nb, L + 8, CONV_DIM), F32), pltpu.VMEM((nb, SSM_STATE, C_M), F32)],
        name="ssd",
        compiler_params=pltpu.CompilerParams(
            dimension_semantics=("parallel", "arbitrary"), vmem_limit_bytes=VMEM_LIMIT),
    )(p_m, *params)


def _fox_prep_kernel(f_ref, fb_ref, nck_ref, carry_ref):
    L = f_ref.shape[0]

    @pl.when(pl.program_id(1) == 0)
    def _():
        carry_ref[...] = jnp.zeros_like(carry_ref)

    log_f = -_softplus(-(f_ref[...] + fb_ref[...]))
    tri = jnp.where(_iota((L, L), 0) >= _iota((L, L), 1), 1.0, 0.0).astype(BF16)
    c = _mm(tri, log_f, 1, 3) + carry_ref[0:1, :]
    carry_ref[...] = jnp.broadcast_to(c[L - 1:L, :], carry_ref.shape)
    nck_ref[...] = (c * (-LOG2E)).T[0:8, :]


def _fox_prep(p_f, f_bias):
    bsz, t, cols = p_f.shape
    L = FOX_BLOCK
    fblk = (cols - LANE) // LANE
    return pl.pallas_call(
        _fox_prep_kernel,
        grid=(bsz, t // L),
        in_specs=[pl.BlockSpec((None, L, LANE), lambda b, c: (b, c, fblk)), _const_spec(f_bias.shape)],
        out_specs=pl.BlockSpec((None, 8, L), lambda b, c: (b, 0, c)),
        out_shape=jax.ShapeDtypeStruct((bsz, 8, t), F32),
        scratch_shapes=[pltpu.VMEM((8, LANE), F32)],
        name="fox_prep",
        compiler_params=pltpu.CompilerParams(
            dimension_semantics=("parallel", "arbitrary"), vmem_limit_bytes=VMEM_LIMIT),
    )(p_f, f_bias)


def _lockstep(gens, offset=0):
    def delayed(g, n):
        for _ in range(n):
            yield
        yield from g
    for _ in itertools.zip_longest(*[delayed(g, b * offset) for b, g in enumerate(gens)]):
        pass


def _fox_kernel(q_ref, k_ref, v_ref, nck_ref, nw_ref, o_ref, q2_ref, m_ref, l_ref, al_ref, p_ref,
                acc_ref):
    i, j = pl.program_id(1), pl.program_id(2)
    nb, tq, tk = q_ref.shape[0], q_ref.shape[1], k_ref.shape[1]
    C = C_F
    zero = jnp.zeros((), BF16)

    def per_head_lanes(xs):
        first = _iota(xs[0].shape, 1) < HEAD_DIM
        return jnp.concatenate([jnp.where(first, xs[0], xs[1]), jnp.where(first, xs[2], xs[3])], axis=1)

    @pl.when(j == 0)
    def _():
        head = _iota((tq, C), 1) // HEAD_DIM
        for b in range(nb):
            q2 = (q_ref[b] * (LOG2E / math.sqrt(HEAD_DIM))).astype(BF16)
            for h in range(H_F):
                q2_ref[b, h] = jnp.where(head == h, q2, zero)
        m_ref[...] = jnp.full(m_ref.shape, NEG_BIG, F32)
        l_ref[...] = jnp.zeros_like(l_ref)
        acc_ref[...] = jnp.zeros_like(acc_ref)

    def step(b, masked):
        kb = k_ref[b].astype(BF16)
        vb = v_ref[b].astype(BF16)
        head = _iota((tk, C), 1) // HEAD_DIM
        nck = nck_ref[b]
        rs = FOX_STRIP
        for h in range(H_F):
            s_all = lax.dot_general(q2_ref[b, h], kb, _NT, preferred_element_type=F32)
            yield
            nk = nck[h:h + 1, :]
            for r in range(0, tq, rs):
                sh = s_all[r:r + rs, :] + nk
                if masked:
                    sh = jnp.where(r + _iota((rs, tk), 0) >= _iota((rs, tk), 1), sh, NEG_BIG)
                m_old = m_ref[b, h, r:r + rs, :]
                m_new = jnp.maximum(m_old, jnp.max(sh, axis=-1, keepdims=True))
                m_ref[b, h, r:r + rs, :] = m_new
                alpha = jnp.exp2(m_old - m_new)
                p = jnp.exp2(sh - jnp.concatenate([m_new] * (tk // LANE), axis=1))
                l_ref[b, h, r:r + rs, :] = (alpha * l_ref[b, h, r:r + rs, :]
                                            + jnp.sum(p, axis=-1, keepdims=True))
                al_ref[b, h, r:r + rs, :] = alpha
                p_ref[b, r:r + rs, h * tk:(h + 1) * tk] = p.astype(BF16)
            yield
            if h % 2 == 1:
                lanes = slice((h - 1) * HEAD_DIM, (h + 1) * HEAD_DIM)
                vpair = jnp.concatenate([jnp.where(head == g, vb, zero)[:, lanes] for g in (h - 1, h)], axis=0)
                first = _iota((tq, LANE), 1) < HEAD_DIM
                alpha = jnp.where(first, al_ref[b, h - 1], al_ref[b, h])
                acc_ref[b, :, lanes] = acc_ref[b, :, lanes] * alpha + jnp.dot(
                    p_ref[b, :, (h - 1) * tk:(h + 1) * tk], vpair, preferred_element_type=F32)
                yield

    @pl.when(j < i)
    def _():
        _lockstep([step(b, False) for b in range(nb)], offset=1)

    @pl.when(j == i)
    def _():
        _lockstep([step(b, True) for b in range(nb)], offset=1)
        seg = ((_iota((C, C), 0) // HEAD_DIM) == (_iota((C, C), 1) // HEAD_DIM))
        seg_mean = jnp.where(seg, 1.0 / HEAD_DIM, 0.0).astype(BF16)
        for b in range(nb):
            y = acc_ref[b] / per_head_lanes([l_ref[b, h] for h in range(H_F)])
            ms = _mm(y * y, seg_mean, 3, 1)
            o_ref[b] = y * lax.rsqrt(ms + NORM_EPS) * nw_ref[...]


def _fox(p_f, nck, norm_w):
    bsz, t, _ = p_f.shape
    blk = FOX_BLOCK
    nq = t // blk
    nb = FOX_ROWS if bsz % FOX_ROWS == 0 else 1
    kv = lambda col: (lambda b, i, j: (b, jnp.minimum(j, i), col))
    return pl.pallas_call(
        _fox_kernel,
        grid=(bsz // nb, nq, nq),
        in_specs=[pl.BlockSpec((nb, blk, C_F), lambda b, i, j: (b, i, 0)),
                  pl.BlockSpec((nb, blk, C_F), kv(1)),
                  pl.BlockSpec((nb, blk, C_F), kv(2)),
                  pl.BlockSpec((nb, 8, blk), lambda b, i, j: (b, 0, jnp.minimum(j, i))),
                  _const_spec(norm_w.shape)],
        out_specs=pl.BlockSpec((nb, blk, C_F), lambda b, i, j: (b, i, 0)),
        out_shape=jax.ShapeDtypeStruct((bsz, t, C_F), F32),
        scratch_shapes=[pltpu.VMEM((nb, H_F, blk, C_F), BF16), pltpu.VMEM((nb, H_F, blk, LANE), F32),
                        pltpu.VMEM((nb, H_F, blk, LANE), F32), pltpu.VMEM((nb, H_F, blk, LANE), F32),
                        pltpu.VMEM((nb, blk, H_F * blk), BF16), pltpu.VMEM((nb, blk, C_F), F32)],
        name="fox_attn",
        compiler_params=pltpu.CompilerParams(
            dimension_semantics=("parallel", "parallel", "arbitrary"), vmem_limit_bytes=VMEM_LIMIT),
    )(p_f, p_f, p_f, nck, norm_w)


def _rms(x, g):
    return x * lax.rsqrt(jnp.mean(x * x, axis=-1, keepdims=True) + NORM_EPS) * g


def _mix_mlp_kernel(h_ref, yr_ref, ym_ref, yf_ref, wor_ref, wom_ref, wof_ref, g1_ref, g2_ref, g3_ref,
                    wu_ref, wd_ref, o_ref):
    tm = h_ref.shape[0]
    halves = [slice(0, tm // 2), slice(tm // 2, tm)]
    dot = functools.partial(jnp.dot, preferred_element_type=F32)
    mix = [dot(yr_ref[r, :].astype(BF16), wor_ref[...]) + dot(ym_ref[r, :].astype(BF16), wom_ref[...])
           + dot(yf_ref[r, :].astype(BF16), wof_ref[...]) for r in halves]
    h1 = [h_ref[r, :] + _rms(m, g1_ref[...]) for r, m in zip(halves, mix)]
    u = [_rms(x, g2_ref[...]).astype(BF16) for x in h1]
    d_ff = wu_ref.shape[1]
    step = 1024
    ff = [None] * len(halves)
    for c0 in range(0, d_ff, step):
        hid = [dot(ui, wu_ref[:, c0:c0 + step]) for ui in u]
        hid = [jnp.square(jnp.maximum(x, 0.0)).astype(BF16) for x in hid]
        part = [dot(x, wd_ref[c0:c0 + step, :]) for x in hid]
        ff = [p if f is None else f + p for f, p in zip(ff, part)]
    for r, x, f in zip(halves, h1, ff):
        o_ref[r, :] = x + _rms(f, g3_ref[...])


def _mix_mlp(h2d, yr, ym, yf, prm, tm):
    n, d = h2d.shape
    names = ("wo_r", "wo_m", "wo_f", "g_mix_post", "g_mlp_pre", "g_mlp_post", "w_up", "w_down")
    params = [prm[nm] for nm in names]
    row = lambda w: pl.BlockSpec((tm, w), lambda i: (i, 0))
    return pl.pallas_call(
        _mix_mlp_kernel,
        grid=(n // tm,),
        in_specs=[row(d), row(yr.shape[1]), row(ym.shape[1]), row(yf.shape[1])]
                 + [_const_spec(x.shape) for x in params],
        out_specs=row(d),
        out_shape=jax.ShapeDtypeStruct((n, d), F32),
        name="mix_mlp",
        compiler_params=pltpu.CompilerParams(
            dimension_semantics=("parallel",), vmem_limit_bytes=VMEM_LIMIT),
    )(h2d, yr, ym, yf, *params)


def _pad_cols(w, width):
    return jnp.pad(w, ((0, 0), (0, width - w.shape[1])))


def _layer_params(l, norm_mix_pre, norm_mix_post, w_in, rwkv_mu, rwkv_w0, rwkv_w2, rwkv_a0, rwkv_a2,
                  rwkv_g2, rwkv_k_k, rwkv_k_a, rwkv_r_k, rwkv_ln_w, rwkv_ln_b, ssm_conv_w, ssm_conv_b,
                  ssm_dt_bias, ssm_A_log, ssm_D, ssm_norm_w, fox_f_bias, fox_norm_w, w_out,
                  norm_mlp_pre, norm_mlp_post, w_mlp_up, w_mlp_down):
    row = lambda x: x.reshape(1, -1).astype(F32)
    lo_rows = R_DECAY + R_AAA + R_GATE
    place = lambda m, r0: jnp.pad(m, ((r0, lo_rows - r0 - m.shape[0]), (0, 0)))
    inproj = dict(g=row(norm_mix_pre[l]))
    rwkv = dict(mu=row(rwkv_mu[l]), w0=row(rwkv_w0[l]), w2=place(rwkv_w2[l], 0), a0=row(rwkv_a0[l]),
                a2=place(rwkv_a2[l], R_DECAY), g2=place(rwkv_g2[l], R_DECAY + R_AAA),
                k_k=row(rwkv_k_k[l]), k_a=row(rwkv_k_a[l]), r_k=row(rwkv_r_k[l]),
                ln_w=row(rwkv_ln_w[l]), ln_b=row(rwkv_ln_b[l]))
    ssd = dict(conv_w=ssm_conv_w[l], conv_b=row(ssm_conv_b[l]),
               dt_bias=_pad_cols(row(ssm_dt_bias[l]), LANE),
               a_neg=_pad_cols(row(-jnp.exp(ssm_A_log[l].astype(F32))), LANE),
               d_x=row(jnp.repeat(ssm_D[l], HEAD_DIM)), norm_w=row(ssm_norm_w[l]))
    fox = dict(f_bias=_pad_cols(row(fox_f_bias[l]), LANE), norm_w=row(fox_norm_w[l]))
    wo = w_out[l].astype(BF16)
    mlp = dict(wo_r=wo[:C_R], wo_m=wo[C_R:C_R + C_M], wo_f=wo[C_R + C_M:],
               g_mix_post=row(norm_mix_post[l]), g_mlp_pre=row(norm_mlp_pre[l]),
               g_mlp_post=row(norm_mlp_post[l]), w_up=w_mlp_up[l].astype(BF16),
               w_down=w_mlp_down[l].astype(BF16))
    return inproj, rwkv, ssd, fox, mlp


def kernel(x, norm_mix_pre, norm_mix_post, w_in, rwkv_mu, rwkv_w0, rwkv_w2, rwkv_a0, rwkv_a2, rwkv_g2, rwkv_k_k, rwkv_k_a, rwkv_r_k, rwkv_ln_w, rwkv_ln_b, ssm_conv_w, ssm_conv_b, ssm_dt_bias, ssm_A_log, ssm_D, ssm_norm_w, fox_f_bias, fox_norm_w, w_out, norm_mlp_pre, norm_mlp_post, w_mlp_up, w_mlp_down):
    weights = (norm_mix_pre, norm_mix_post, w_in, rwkv_mu, rwkv_w0, rwkv_w2, rwkv_a0, rwkv_a2, rwkv_g2,
               rwkv_k_k, rwkv_k_a, rwkv_r_k, rwkv_ln_w, rwkv_ln_b, ssm_conv_w, ssm_conv_b, ssm_dt_bias,
               ssm_A_log, ssm_D, ssm_norm_w, fox_f_bias, fox_norm_w, w_out, norm_mlp_pre, norm_mlp_post,
               w_mlp_up, w_mlp_down)
    bsz, t, d = x.shape
    n = bsz * t
    assert t % FOX_BLOCK == 0 and t % SSD_CHUNK == 0 and t % RWKV_CHUNK == 0
    tm = 512 if n % 512 == 0 else 256
    h = x.reshape(n, d)
    w_proj = _inproj_weights(w_in)
    for l in range(w_in.shape[0]):
        inproj, rwkv, ssd, fox, mlp = _layer_params(l, *weights)
        p_r, p_m, p_f = _inproj(h, inproj["g"], w_proj[l], tm)
        y_r = _rwkv(p_r.reshape(bsz, t, -1), rwkv)
        y_m = _ssd(p_m.reshape(bsz, t, -1), ssd)
        p_f3 = p_f.reshape(bsz, t, -1)
        nck = _fox_prep(p_f3, fox["f_bias"])
        y_f = _fox(p_f3, nck, fox["norm_w"])
        h = _mix_mlp(h, y_r.reshape(n, -1), y_m.reshape(n, -1), y_f.reshape(n, -1), mlp, tm)
    return h.reshape(bsz, t, d)
```

```python
import functools
import itertools
import math

import jax
import jax.numpy as jnp
from jax import lax
from jax.experimental import pallas as pl
from jax.experimental.pallas import tpu as pltpu

F32 = jnp.float32
BF16 = jnp.bfloat16

HEAD_DIM = 64
NORM_EPS = 1e-6
C_R = 256
R_DECAY, R_AAA, R_GATE = 32, 32, 64
RWKV_GN_EPS = 64e-5
RWKV_COLS = 3 * C_R + R_DECAY + R_AAA + R_GATE
RWKV_CHUNK = 64
RWKV_ROWS = 8
C_M = 512
H_M = 8
SSM_STATE = 128
SSM_GROUPS = 2
CONV_K = 4
CONV_DIM = C_M + 2 * SSM_GROUPS * SSM_STATE
SSD_CHUNK = 128
SSD_ROWS = 8
SSM_NORM_EPS = 1e-5
SSM_COLS_PAD = C_M + CONV_DIM + 128
C_F = 256
H_F = 4
FOX_COLS_PAD = 3 * C_F + 128
FOX_BLOCK = 512
FOX_STRIP = 32
FOX_ROWS = 2
FOX_PREP_ROWS = 8
LOG2E = 1.4426950408889634
LANE = 128
NEG_BIG = -1e30

VMEM_LIMIT = 48 * 1024 * 1024

_NN = (((1,), (0,)), ((), ()))
_NT = (((1,), (1,)), ((), ()))


def _split(x, n):
    if x.dtype == BF16:
        return [x]
    terms = []
    rem = x
    for i in range(n):
        t = rem.astype(BF16)
        terms.append(t)
        if i + 1 < n:
            rem = rem - t.astype(F32)
    return terms


def _mm(a, b, na=1, nb=1, dims=_NN):
    ta, tb = _split(a, na), _split(b, nb)
    depth = max(len(ta), len(tb))
    out = None
    for i, x in enumerate(ta):
        for j, y in enumerate(tb):
            if i + j < depth:
                d = lax.dot_general(x, y, dims, preferred_element_type=F32)
                out = d if out is None else out + d
    return out


def _sigmoid(x):
    return 1.0 / (1.0 + jnp.exp(-x))


def _softplus(x):
    return jnp.maximum(x, 0.0) + jnp.log1p(jnp.exp(-jnp.abs(x)))


def _iota(shape, dim):
    return lax.broadcasted_iota(jnp.int32, shape, dim)


def _block_diag(x, nblk, rblk, cblk):
    t = jnp.concatenate([x.astype(BF16)] * nblk, axis=0)
    shape = t.shape
    keep = (_iota(shape, 0) // rblk) == (_iota(shape, 1) // cblk)
    return jnp.where(keep, t, jnp.zeros((), t.dtype))


def _inproj_kernel(h_ref, g_ref, w_ref, pr_ref, pm_ref, pf_ref):
    x = h_ref[...]
    ms = jnp.mean(x * x, axis=-1, keepdims=True)
    u = (x * lax.rsqrt(ms + NORM_EPS) * g_ref[...]).astype(BF16)
    proj = jnp.dot(u, w_ref[...], preferred_element_type=F32)
    c0 = 0
    for o_ref in (pr_ref, pm_ref, pf_ref):
        c1 = c0 + o_ref.shape[1]
        o_ref[...] = proj[:, c0:c1]
        c0 = c1


def _const_spec(shape):
    nd = len(shape)
    return pl.BlockSpec(shape, lambda *_: (0,) * nd, pipeline_mode=pl.Buffered(1))


def _inproj_weights(w_in):
    ssm_end = RWKV_COLS + C_M + CONV_DIM + H_M
    cols = w_in.shape[-1]
    width = RWKV_COLS + SSM_COLS_PAD + FOX_COLS_PAD
    src = jnp.arange(cols, dtype=jnp.int32)
    dst = jnp.where(src < ssm_end, src, src - ssm_end + RWKV_COLS + SSM_COLS_PAD)
    place = (dst[:, None] == jnp.arange(width, dtype=jnp.int32)[None, :]).astype(BF16)
    return jnp.einsum("lkc,cd->lkd", w_in.astype(BF16), place, preferred_element_type=BF16)


def _inproj(h2d, g, w, tm):
    n, d = h2d.shape
    widths = (RWKV_COLS, SSM_COLS_PAD, FOX_COLS_PAD)
    assert w.shape[1] == sum(widths)
    return pl.pallas_call(
        _inproj_kernel,
        grid=(n // tm,),
        in_specs=[pl.BlockSpec((tm, d), lambda i: (i, 0)), _const_spec(g.shape), _const_spec(w.shape)],
        out_specs=[pl.BlockSpec((tm, c), lambda i: (i, 0)) for c in widths],
        out_shape=[jax.ShapeDtypeStruct((n, c), F32) for c in widths],
        name="inproj",
        compiler_params=pltpu.CompilerParams(
            dimension_semantics=("parallel",), vmem_limit_bytes=VMEM_LIMIT),
    )(h2d, g, w)


def _rwkv_kernel(p_ref, mu_ref, w0_ref, w2_ref, a0_ref, a2_ref, g2_ref, kk_ref, ka_ref,
                 rk_ref, lnw_ref, lnb_ref, tri_ref, o_ref, s_ref, prev_ref):
    nb, L, C = p_ref.shape[0], RWKV_CHUNK, C_R
    nh = C // HEAD_DIM
    R = nb * L

    @pl.when(pl.program_id(1) == 0)
    def _():
        s_ref[...] = jnp.zeros_like(s_ref)
        prev_ref[...] = jnp.zeros_like(prev_ref)

    p = jnp.concatenate([p_ref[bi] for bi in range(nb)], axis=0)
    row = _iota((R, 1), 0)
    shifted = pltpu.roll(p, 1, 0)
    for bi in range(nb):
        shifted = jnp.where(row == bi * L, prev_ref[bi, 7:8, :], shifted)
        prev_ref[bi] = p_ref[bi, L - 8:L, :]
    xs = p + (shifted - p) * mu_ref[...]
    r, k, v, lo = xs[:, :C], xs[:, C:2 * C], xs[:, 2 * C:3 * C], xs[:, 3 * C:]

    seg = ((_iota((C, C), 0) // HEAD_DIM) == (_iota((C, C), 1) // HEAD_DIM))
    seg_ones = jnp.where(seg, 1.0, 0.0).astype(BF16)
    seg_mean = jnp.where(seg, 1.0 / HEAD_DIM, 0.0).astype(BF16)

    logw = w0_ref[...] + _mm(jnp.tanh(lo), w2_ref[...], 2, 2)
    lw = -math.exp(-0.5) * _sigmoid(logw)
    lr = _sigmoid(a0_ref[...] + _mm(lo, a2_ref[...]))
    gate = _mm(_sigmoid(lo), g2_ref[...])
    kk = k * kk_ref[...]
    kk = kk / jnp.maximum(jnp.sqrt(_mm(kk * kk, seg_ones)), 1e-12)
    k = k * (1.0 + (lr - 1.0) * ka_ref[...])
    a = -kk
    b = kk * lr

    cum = _mm(tri_ref[...], lw, 1, 3)
    cl = jnp.concatenate([jnp.broadcast_to(cum[(bi + 1) * L - 1:(bi + 1) * L, :], (L, C))
                          for bi in range(nb)], axis=0)
    e_neg = jnp.exp(-cum)
    e_end = jnp.exp(cl - cum)
    chunks = lambda x: [x[bi * L:(bi + 1) * L] for bi in range(nb)]
    At, Rt, Kt, Bt = chunks(a * jnp.exp(cum - lw)), chunks(r * jnp.exp(cum)), chunks(k * e_neg), chunks(b * e_neg)
    Bp, Kp, vs = chunks(b * e_end), chunks(k * e_end), chunks(v)
    e_tot = [x[L - 1:L, :] for x in chunks(jnp.exp(cl))]

    bd = functools.partial(_block_diag, nblk=nh, rblk=L, cblk=HEAD_DIM)
    rows = range(nb)
    lhs = [jnp.concatenate([At[i], Rt[i]], axis=0) for i in rows]
    m1 = [_mm(lhs[i], jnp.concatenate([bd(Bt[i]), bd(Kt[i])], axis=0), dims=_NT) for i in rows]
    tcol = _iota((L, C), 1) % HEAD_DIM
    trow = _iota((L, C), 0)
    strict = tcol < trow
    incl = tcol <= trow
    a_ak = [jnp.where(strict, m[:L, C:], 0.0) for m in m1]
    r_rb = [jnp.where(incl, m[L:, :C], 0.0) for m in m1]
    r_rk = [jnp.where(incl, m[L:, C:], 0.0) for m in m1]

    x = [jnp.where(strict, m[:L, :C], 0.0) for m in m1]
    eye = jnp.where(tcol == trow, 1.0, 0.0)
    t_inv = [eye + xi for xi in x]
    x = [_mm(xi, bd(xi)) for xi in x]
    for _ in range(int(math.log2(L)) - 2):
        both = [_mm(jnp.concatenate([ti, xi], axis=0), bd(xi)) for ti, xi in zip(t_inv, x)]
        t_inv = [ti + b[:L] for ti, b in zip(t_inv, both)]
        x = [b[L:] for b in both]
    t_inv = [ti + _mm(ti, bd(xi)) for ti, xi in zip(t_inv, x)]

    s0 = [s_ref[i] for i in rows]
    from_s = [_mm(lhs[i], s0[i], dims=_NT) for i in rows]
    from_v = [_mm(jnp.concatenate([a_ak[i], r_rk[i]], axis=0), bd(vs[i])) for i in rows]
    u = [_mm(t_inv[i], bd(from_s[i][:L] + from_v[i][:L])) for i in rows]
    y = [from_s[i][L:] + from_v[i][L:] + _mm(r_rb[i], bd(u[i])) for i in rows]
    upd = [_mm(jnp.concatenate([u[i], vs[i]], axis=0).T,
               jnp.concatenate([Bp[i], Kp[i]], axis=0)) for i in rows]
    for i in rows:
        s_ref[i] = s0[i] * e_tot[i] + jnp.where(seg, upd[i], 0.0)

    y = jnp.concatenate(y, axis=0)
    mean = _mm(y, seg_mean)
    d = y - mean
    var = _mm(d * d, seg_mean)
    yn = d * lax.rsqrt(var + RWKV_GN_EPS) * lnw_ref[...] + lnb_ref[...]
    bonus = _mm(r * k * rk_ref[...], seg_ones) * v
    out = (yn + bonus) * gate
    for i in rows:
        o_ref[i] = out[i * L:(i + 1) * L]


def _rwkv(p_r, prm):
    bsz, t, cols = p_r.shape
    L = RWKV_CHUNK
    nb = RWKV_ROWS if bsz % RWKV_ROWS == 0 else 1
    names = ("mu", "w0", "w2", "a0", "a2", "g2", "k_k", "k_a", "r_k", "ln_w", "ln_b")
    rr, cc = _iota((nb * L, nb * L), 0), _iota((nb * L, nb * L), 1)
    tri = jnp.where((rr >= cc) & (rr // L == cc // L), 1.0, 0.0).astype(BF16)
    params = [prm[nm] for nm in names] + [tri]
    return pl.pallas_call(
        _rwkv_kernel,
        grid=(bsz // nb, t // L),
        in_specs=[pl.BlockSpec((nb, L, cols), lambda b, c: (b, c, 0))]
                 + [_const_spec(x.shape) for x in params],
        out_specs=pl.BlockSpec((nb, L, C_R), lambda b, c: (b, c, 0)),
        out_shape=jax.ShapeDtypeStruct((bsz, t, C_R), F32),
        scratch_shapes=[pltpu.VMEM((nb, C_R, C_R), F32), pltpu.VMEM((nb, 8, cols), F32)],
        name="rwkv7",
        compiler_params=pltpu.CompilerParams(
            dimension_semantics=("parallel", "arbitrary"), vmem_limit_bytes=VMEM_LIMIT),
    )(p_r, *params)


def _ssd_kernel(p_ref, cw_ref, cb_ref, dtb_ref, a_ref, d_ref, nw_ref, o_ref, xe_ref, s_ref):
    rows = [_ssd_row(p_ref.at[b], cw_ref, cb_ref, dtb_ref, a_ref, d_ref, nw_ref, o_ref.at[b], xe_ref.at[b],
                     s_ref.at[b]) for b in range(p_ref.shape[0])]
    _lockstep(rows)


def _ssd_row(p_ref, cw_ref, cb_ref, dtb_ref, a_ref, d_ref, nw_ref, o_ref, xe_ref, s_ref):
    L, N = SSD_CHUNK, SSM_STATE
    hg = H_M // SSM_GROUPS
    gw = hg * HEAD_DIM

    @pl.when(pl.program_id(1) == 0)
    def _():
        s_ref[...] = jnp.zeros_like(s_ref)
        xe_ref[0:8, :] = jnp.zeros((8, CONV_DIM), F32)

    @pl.when(pl.program_id(1) > 0)
    def _():
        xe_ref[0:8, :] = xe_ref[L:L + 8, :]

    p = p_ref[...]
    z = p[:, :C_M]
    xbc = p[:, C_M:C_M + CONV_DIM]
    xe_ref[8:8 + L, :] = xbc
    cw = cw_ref[...]
    conv = cb_ref[...] + xbc * cw[CONV_K - 1:CONV_K, :]
    for s in range(1, CONV_K):
        conv = conv + xe_ref[8 - s:8 - s + L, :] * cw[CONV_K - 1 - s:CONV_K - s, :]
    act = conv * _sigmoid(conv)
    yield
    xs = act[:, :C_M]
    bm = act[:, C_M:C_M + SSM_GROUPS * N]
    cm = act[:, C_M + SSM_GROUPS * N:]

    dt = _softplus(p[:, C_M + CONV_DIM:] + dtb_ref[...])
    a = dt * a_ref[...]
    tri = jnp.where(_iota((L, L), 0) >= _iota((L, L), 1), 1.0, 0.0).astype(BF16)
    acum = _mm(tri, a, 1, 3)
    yield
    acum_cols = [jnp.broadcast_to(acum[:, h:h + 1], (L, LANE)) for h in range(H_M)]
    dt_cols = [jnp.broadcast_to(dt[:, h:h + 1], (L, LANE)) for h in range(H_M)]
    first = _iota((L, LANE), 1) < HEAD_DIM
    pairs = lambda cols: jnp.concatenate(
        [jnp.where(first, cols[h], cols[h + 1]) for h in range(0, H_M, 2)], axis=1)
    acum_x = pairs(acum_cols)
    dt_x = pairs(dt_cols)
    acum_row = acum.T
    atot_x = acum_x[L - 1:L, :]
    xdt = xs * dt_x
    x_end = xdt * jnp.exp(atot_x - acum_x)
    e_in = jnp.exp(acum_x)
    e_tot = jnp.exp(atot_x)
    causal = _iota((L, L), 0) >= _iota((L, L), 1)
    yield

    s_old = s_ref[...]
    ys = []
    for g in range(SSM_GROUPS):
        bg = bm[:, g * N:(g + 1) * N]
        cg = cm[:, g * N:(g + 1) * N]
        cb = _mm(cg, bg, 1, 1, _NT)
        yield
        lmats = []
        for e in range(hg):
            h = g * hg + e
            diff = acum_cols[h] - acum_row[h:h + 1, :]
            lmats.append(cb * jnp.exp(jnp.where(causal, diff, NEG_BIG)))
        gs = slice(g * gw, (g + 1) * gw)
        yield
        y_diag = _mm(jnp.concatenate(lmats, axis=1), _block_diag(xdt[:, gs], hg, L, HEAD_DIM))
        sg = s_old[:, gs]
        y_off = _mm(cg, sg) * e_in[:, gs]
        s_ref[:, gs] = sg * e_tot[:, gs] + _mm(bg.T, x_end[:, gs])
        ys.append(y_diag + y_off)
        yield
    y = jnp.concatenate(ys, axis=1) + d_ref[...] * xs
    y = y * (z * _sigmoid(z))
    yield
    outs = []
    for g in range(SSM_GROUPS):
        yg = y[:, g * gw:(g + 1) * gw]
        outs.append(yg * lax.rsqrt(jnp.mean(yg * yg, axis=-1, keepdims=True) + SSM_NORM_EPS))
    o_ref[...] = jnp.concatenate(outs, axis=1) * nw_ref[...]
    yield


def _ssd(p_m, prm):
    bsz, t, cols = p_m.shape
    L = SSD_CHUNK
    nb = SSD_ROWS if bsz % SSD_ROWS == 0 else 1
    names = ("conv_w", "conv_b", "dt_bias", "a_neg", "d_x", "norm_w")
    params = [prm[nm] for nm in names]
    return pl.pallas_call(
        _ssd_kernel,
        grid=(bsz // nb, t // L),
        in_specs=[pl.BlockSpec((nb, L, cols), lambda b, c: (b, c, 0))]
                 + [_const_spec(x.shape) for x in params],
        out_specs=pl.BlockSpec((nb, L, C_M), lambda b, c: (b, c, 0)),
        out_shape=jax.ShapeDtypeStruct((bsz, t, C_M), F32),
        scratch_shapes=[pltpu.VMEM((nb, L + 8, CONV_DIM), F32), pltpu.VMEM((nb, SSM_STATE, C_M), F32)],
        name="ssd",
        compiler_params=pltpu.CompilerParams(
            dimension_semantics=("parallel", "arbitrary"), vmem_limit_bytes=VMEM_LIMIT),
    )(p_m, *params)


def _fox_prep_kernel(f_ref, fb_ref, nck_ref, carry_ref):
    nb, L = f_ref.shape[0], f_ref.shape[1]

    @pl.when(pl.program_id(1) == 0)
    def _():
        carry_ref[...] = jnp.zeros_like(carry_ref)

    tri = jnp.where(_iota((L, L), 0) >= _iota((L, L), 1), 1.0, 0.0).astype(BF16)
    for b in range(nb):
        log_f = -_softplus(-(f_ref[b] + fb_ref[...]))
        c = _mm(tri, log_f, 1, 3) + carry_ref[b, 0:1, :]
        carry_ref[b] = jnp.broadcast_to(c[L - 1:L, :], carry_ref.shape[1:])
        nck_ref[b] = (c * (-LOG2E)).T[0:8, :]


def _fox_prep(p_f, f_bias):
    bsz, t, cols = p_f.shape
    L = FOX_BLOCK
    nb = FOX_PREP_ROWS if bsz % FOX_PREP_ROWS == 0 else 1
    fblk = (cols - LANE) // LANE
    return pl.pallas_call(
        _fox_prep_kernel,
        grid=(bsz // nb, t // L),
        in_specs=[pl.BlockSpec((nb, L, LANE), lambda b, c: (b, c, fblk)), _const_spec(f_bias.shape)],
        out_specs=pl.BlockSpec((nb, 8, L), lambda b, c: (b, 0, c)),
        out_shape=jax.ShapeDtypeStruct((bsz, 8, t), F32),
        scratch_shapes=[pltpu.VMEM((nb, 8, LANE), F32)],
        name="fox_prep",
        compiler_params=pltpu.CompilerParams(
            dimension_semantics=("parallel", "arbitrary"), vmem_limit_bytes=VMEM_LIMIT),
    )(p_f, f_bias)


def _lockstep(gens, offset=0):
    def delayed(g, n):
        for _ in range(n):
            yield
        yield from g
    for _ in itertools.zip_longest(*[delayed(g, b * offset) for b, g in enumerate(gens)]):
        pass


def _fox_kernel(q_ref, k_ref, v_ref, nck_ref, nw_ref, o_ref, q2_ref, m_ref, l_ref, al_ref, p_ref,
                acc_ref):
    i, j = pl.program_id(1), pl.program_id(2)
    nb, tq, tk = q_ref.shape[0], q_ref.shape[1], k_ref.shape[1]
    C = C_F
    zero = jnp.zeros((), BF16)

    def per_head_lanes(xs):
        first = _iota(xs[0].shape, 1) < HEAD_DIM
        return jnp.concatenate([jnp.where(first, xs[0], xs[1]), jnp.where(first, xs[2], xs[3])], axis=1)

    @pl.when(j == 0)
    def _():
        head = _iota((tq, C), 1) // HEAD_DIM
        for b in range(nb):
            q2 = (q_ref[b] * (LOG2E / math.sqrt(HEAD_DIM))).astype(BF16)
            for h in range(H_F):
                q2_ref[b, h] = jnp.where(head == h, q2, zero)
        m_ref[...] = jnp.full(m_ref.shape, NEG_BIG, F32)
        l_ref[...] = jnp.zeros_like(l_ref)
        acc_ref[...] = jnp.zeros_like(acc_ref)

    def step(b, masked):
        kb = k_ref[b].astype(BF16)
        vb = v_ref[b].astype(BF16)
        head = _iota((tk, C), 1) // HEAD_DIM
        nck = nck_ref[b]
        rs = FOX_STRIP
        for h in range(H_F):
            s_all = lax.dot_general(q2_ref[b, h], kb, _NT, preferred_element_type=F32)
            yield
            nk = nck[h:h + 1, :]
            for r in range(0, tq, rs):
                sh = s_all[r:r + rs, :] + nk
                if masked:
                    sh = jnp.where(r + _iota((rs, tk), 0) >= _iota((rs, tk), 1), sh, NEG_BIG)
                m_old = m_ref[b, h, r:r + rs, :]
                m_new = jnp.maximum(m_old, jnp.max(sh, axis=-1, keepdims=True))
                m_ref[b, h, r:r + rs, :] = m_new
                alpha = jnp.exp2(m_old - m_new)
                p = jnp.exp2(sh - jnp.concatenate([m_new] * (tk // LANE), axis=1))
                l_ref[b, h, r:r + rs, :] = (alpha * l_ref[b, h, r:r + rs, :]
                                            + jnp.sum(p, axis=-1, keepdims=True))
                al_ref[b, h, r:r + rs, :] = alpha
                p_ref[b, r:r + rs, h * tk:(h + 1) * tk] = p.astype(BF16)
            yield
            if h % 2 == 1:
                lanes = slice((h - 1) * HEAD_DIM, (h + 1) * HEAD_DIM)
                vpair = jnp.concatenate([jnp.where(head == g, vb, zero)[:, lanes] for g in (h - 1, h)], axis=0)
                first = _iota((tq, LANE), 1) < HEAD_DIM
                alpha = jnp.where(first, al_ref[b, h - 1], al_ref[b, h])
                acc_ref[b, :, lanes] = acc_ref[b, :, lanes] * alpha + jnp.dot(
                    p_ref[b, :, (h - 1) * tk:(h + 1) * tk], vpair, preferred_element_type=F32)
                yield

    @pl.when(j < i)
    def _():
        _lockstep([step(b, False) for b in range(nb)], offset=1)

    @pl.when(j == i)
    def _():
        _lockstep([step(b, True) for b in range(nb)], offset=1)
        seg = ((_iota((C, C), 0) // HEAD_DIM) == (_iota((C, C), 1) // HEAD_DIM))
        seg_mean = jnp.where(seg, 1.0 / HEAD_DIM, 0.0).astype(BF16)
        for b in range(nb):
            y = acc_ref[b] / per_head_lanes([l_ref[b, h] for h in range(H_F)])
            ms = _mm(y * y, seg_mean, 3, 1)
            o_ref[b] = y * lax.rsqrt(ms + NORM_EPS) * nw_ref[...]


def _fox(p_f, nck, norm_w):
    bsz, t, _ = p_f.shape
    blk = FOX_BLOCK
    nq = t // blk
    nb = FOX_ROWS if bsz % FOX_ROWS == 0 else 1
    kv = lambda col: (lambda b, i, j: (b, jnp.minimum(j, i), col))
    return pl.pallas_call(
        _fox_kernel,
        grid=(bsz // nb, nq, nq),
        in_specs=[pl.BlockSpec((nb, blk, C_F), lambda b, i, j: (b, i, 0)),
                  pl.BlockSpec((nb, blk, C_F), kv(1)),
                  pl.BlockSpec((nb, blk, C_F), kv(2)),
                  pl.BlockSpec((nb, 8, blk), lambda b, i, j: (b, 0, jnp.minimum(j, i))),
                  _const_spec(norm_w.shape)],
        out_specs=pl.BlockSpec((nb, blk, C_F), lambda b, i, j: (b, i, 0)),
        out_shape=jax.ShapeDtypeStruct((bsz, t, C_F), F32),
        scratch_shapes=[pltpu.VMEM((nb, H_F, blk, C_F), BF16), pltpu.VMEM((nb, H_F, blk, LANE), F32),
                        pltpu.VMEM((nb, H_F, blk, LANE), F32), pltpu.VMEM((nb, H_F, blk, LANE), F32),
                        pltpu.VMEM((nb, blk, H_F * blk), BF16), pltpu.VMEM((nb, blk, C_F), F32)],
        name="fox_attn",
        compiler_params=pltpu.CompilerParams(
            dimension_semantics=("parallel", "parallel", "arbitrary"), vmem_limit_bytes=VMEM_LIMIT),
    )(p_f, p_f, p_f, nck, norm_w)


def _rms(x, g):
    return x * lax.rsqrt(jnp.mean(x * x, axis=-1, keepdims=True) + NORM_EPS) * g


def _mix_mlp_kernel(h_ref, yr_ref, ym_ref, yf_ref, wor_ref, wom_ref, wof_ref, g1_ref, g2_ref, g3_ref,
                    wu_ref, wd_ref, o_ref):
    tm = h_ref.shape[0]
    halves = [slice(0, tm // 2), slice(tm // 2, tm)]
    dot = functools.partial(jnp.dot, preferred_element_type=F32)
    mix = [dot(yr_ref[r, :].astype(BF16), wor_ref[...]) + dot(ym_ref[r, :].astype(BF16), wom_ref[...])
           + dot(yf_ref[r, :].astype(BF16), wof_ref[...]) for r in halves]
    h1 = [h_ref[r, :] + _rms(m, g1_ref[...]) for r, m in zip(halves, mix)]
    u = [_rms(x, g2_ref[...]).astype(BF16) for x in h1]
    d_ff = wu_ref.shape[1]
    step = 1024
    ff = [None] * len(halves)
    for c0 in range(0, d_ff, step):
        hid = [dot(ui, wu_ref[:, c0:c0 + step]) for ui in u]
        hid = [jnp.square(jnp.maximum(x, 0.0)).astype(BF16) for x in hid]
        part = [dot(x, wd_ref[c0:c0 + step, :]) for x in hid]
        ff = [p if f is None else f + p for f, p in zip(ff, part)]
    for r, x, f in zip(halves, h1, ff):
        o_ref[r, :] = x + _rms(f, g3_ref[...])


def _mix_mlp(h2d, yr, ym, yf, prm, tm):
    n, d = h2d.shape
    names = ("wo_r", "wo_m", "wo_f", "g_mix_post", "g_mlp_pre", "g_mlp_post", "w_up", "w_down")
    params = [prm[nm] for nm in names]
    row = lambda w: pl.BlockSpec((tm, w), lambda i: (i, 0))
    return pl.pallas_call(
        _mix_mlp_kernel,
        grid=(n // tm,),
        in_specs=[row(d), row(yr.shape[1]), row(ym.shape[1]), row(yf.shape[1])]
                 + [_const_spec(x.shape) for x in params],
        out_specs=row(d),
        out_shape=jax.ShapeDtypeStruct((n, d), F32),
        name="mix_mlp",
        compiler_params=pltpu.CompilerParams(
            dimension_semantics=("parallel",), vmem_limit_bytes=VMEM_LIMIT),
    )(h2d, yr, ym, yf, *params)


def _pad_cols(w, width):
    return jnp.pad(w, ((0, 0), (0, width - w.shape[1])))


def _layer_params(l, norm_mix_pre, norm_mix_post, w_in, rwkv_mu, rwkv_w0, rwkv_w2, rwkv_a0, rwkv_a2,
                  rwkv_g2, rwkv_k_k, rwkv_k_a, rwkv_r_k, rwkv_ln_w, rwkv_ln_b, ssm_conv_w, ssm_conv_b,
                  ssm_dt_bias, ssm_A_log, ssm_D, ssm_norm_w, fox_f_bias, fox_norm_w, w_out,
                  norm_mlp_pre, norm_mlp_post, w_mlp_up, w_mlp_down):
    row = lambda x: x.reshape(1, -1).astype(F32)
    lo_rows = R_DECAY + R_AAA + R_GATE
    place = lambda m, r0: jnp.pad(m, ((r0, lo_rows - r0 - m.shape[0]), (0, 0)))
    inproj = dict(g=row(norm_mix_pre[l]))
    rwkv = dict(mu=row(rwkv_mu[l]), w0=row(rwkv_w0[l]), w2=place(rwkv_w2[l], 0), a0=row(rwkv_a0[l]),
                a2=place(rwkv_a2[l], R_DECAY), g2=place(rwkv_g2[l], R_DECAY + R_AAA),
                k_k=row(rwkv_k_k[l]), k_a=row(rwkv_k_a[l]), r_k=row(rwkv_r_k[l]),
                ln_w=row(rwkv_ln_w[l]), ln_b=row(rwkv_ln_b[l]))
    ssd = dict(conv_w=ssm_conv_w[l], conv_b=row(ssm_conv_b[l]),
               dt_bias=_pad_cols(row(ssm_dt_bias[l]), LANE),
               a_neg=_pad_cols(row(-jnp.exp(ssm_A_log[l].astype(F32))), LANE),
               d_x=row(jnp.repeat(ssm_D[l], HEAD_DIM)), norm_w=row(ssm_norm_w[l]))
    fox = dict(f_bias=_pad_cols(row(fox_f_bias[l]), LANE), norm_w=row(fox_norm_w[l]))
    wo = w_out[l].astype(BF16)
    mlp = dict(wo_r=wo[:C_R], wo_m=wo[C_R:C_R + C_M], wo_f=wo[C_R + C_M:],
               g_mix_post=row(norm_mix_post[l]), g_mlp_pre=row(norm_mlp_pre[l]),
               g_mlp_post=row(norm_mlp_post[l]), w_up=w_mlp_up[l].astype(BF16),
               w_down=w_mlp_down[l].astype(BF16))
    return inproj, rwkv, ssd, fox, mlp


def kernel(x, norm_mix_pre, norm_mix_post, w_in, rwkv_mu, rwkv_w0, rwkv_w2, rwkv_a0, rwkv_a2, rwkv_g2, rwkv_k_k, rwkv_k_a, rwkv_r_k, rwkv_ln_w, rwkv_ln_b, ssm_conv_w, ssm_conv_b, ssm_dt_bias, ssm_A_log, ssm_D, ssm_norm_w, fox_f_bias, fox_norm_w, w_out, norm_mlp_pre, norm_mlp_post, w_mlp_up, w_mlp_down):
    weights = (norm_mix_pre, norm_mix_post, w_in, rwkv_mu, rwkv_w0, rwkv_w2, rwkv_a0, rwkv_a2, rwkv_g2,
               rwkv_k_k, rwkv_k_a, rwkv_r_k, rwkv_ln_w, rwkv_ln_b, ssm_conv_w, ssm_conv_b, ssm_dt_bias,
               ssm_A_log, ssm_D, ssm_norm_w, fox_f_bias, fox_norm_w, w_out, norm_mlp_pre, norm_mlp_post,
               w_mlp_up, w_mlp_down)
    bsz, t, d = x.shape
    n = bsz * t
    assert t % FOX_BLOCK == 0 and t % SSD_CHUNK == 0 and t % RWKV_CHUNK == 0
    tm = 512 if n % 512 == 0 else 256
    h = x.reshape(n, d)
    w_proj = _inproj_weights(w_in)
    for l in range(w_in.shape[0]):
        inproj, rwkv, ssd, fox, mlp = _layer_params(l, *weights)
        p_r, p_m, p_f = _inproj(h, inproj["g"], w_proj[l], tm)
        y_r = _rwkv(p_r.reshape(bsz, t, -1), rwkv)
        y_m = _ssd(p_m.reshape(bsz, t, -1), ssd)
        p_f3 = p_f.reshape(bsz, t, -1)
        nck = _fox_prep(p_f3, fox["f_bias"])
        y_f = _fox(p_f3, nck, fox["norm_w"])
        h = _mix_mlp(h, y_r.reshape(n, -1), y_m.reshape(n, -1), y_f.reshape(n, -1), mlp, tm)
    return h.reshape(bsz, t, d)
```

```python
import functools
import itertools
import math

import jax
import jax.numpy as jnp
from jax import lax
from jax.experimental import pallas as pl
from jax.experimental.pallas import tpu as pltpu

F32 = jnp.float32
BF16 = jnp.bfloat16

HEAD_DIM = 64
NORM_EPS = 1e-6
C_R = 256
R_DECAY, R_AAA, R_GATE = 32, 32, 64
RWKV_GN_EPS = 64e-5
RWKV_COLS = 3 * C_R + R_DECAY + R_AAA + R_GATE
RWKV_CHUNK = 64
RWKV_ROWS = 8
C_M = 512
H_M = 8
SSM_STATE = 128
SSM_GROUPS = 2
CONV_K = 4
CONV_DIM = C_M + 2 * SSM_GROUPS * SSM_STATE
SSD_CHUNK = 128
SSD_ROWS = 8
SSM_NORM_EPS = 1e-5
SSM_COLS_PAD = C_M + CONV_DIM + 128
C_F = 256
H_F = 4
FOX_COLS_PAD = 3 * C_F + 128
FOX_BLOCK = 512
FOX_STRIP = 32
FOX_ROWS = 2
FOX_PREP_ROWS = 8
LOG2E = 1.4426950408889634
LANE = 128
NEG_BIG = -1e30

VMEM_LIMIT = 48 * 1024 * 1024

_NN = (((1,), (0,)), ((), ()))
_NT = (((1,), (1,)), ((), ()))


def _split(x, n):
    if x.dtype == BF16:
        return [x]
    terms = []
    rem = x
    for i in range(n):
        t = rem.astype(BF16)
        terms.append(t)
        if i + 1 < n:
            rem = rem - t.astype(F32)
    return terms


def _mm(a, b, na=1, nb=1, dims=_NN):
    ta, tb = _split(a, na), _split(b, nb)
    depth = max(len(ta), len(tb))
    out = None
    for i, x in enumerate(ta):
        for j, y in enumerate(tb):
            if i + j < depth:
                d = lax.dot_general(x, y, dims, preferred_element_type=F32)
                out = d if out is None else out + d
    return out


def _sigmoid(x):
    return 1.0 / (1.0 + jnp.exp(-x))


def _softplus(x):
    return jnp.maximum(x, 0.0) + jnp.log1p(jnp.exp(-jnp.abs(x)))


def _iota(shape, dim):
    return lax.broadcasted_iota(jnp.int32, shape, dim)


def _block_diag(x, nblk, rblk, cblk):
    t = jnp.concatenate([x.astype(BF16)] * nblk, axis=0)
    shape = t.shape
    keep = (_iota(shape, 0) // rblk) == (_iota(shape, 1) // cblk)
    return jnp.where(keep, t, jnp.zeros((), t.dtype))


def _inproj_kernel(h_ref, g_ref, w_ref, pr_ref, pm_ref, pf_ref):
    x = h_ref[...]
    ms = jnp.mean(x * x, axis=-1, keepdims=True)
    u = (x * lax.rsqrt(ms + NORM_EPS) * g_ref[...]).astype(BF16)
    proj = jnp.dot(u, w_ref[...], preferred_element_type=F32)
    c0 = 0
    for o_ref in (pr_ref, pm_ref, pf_ref):
        c1 = c0 + o_ref.shape[1]
        o_ref[...] = proj[:, c0:c1]
        c0 = c1


def _const_spec(shape):
    nd = len(shape)
    return pl.BlockSpec(shape, lambda *_: (0,) * nd, pipeline_mode=pl.Buffered(1))


def _inproj_weights(w_in):
    ssm_end = RWKV_COLS + C_M + CONV_DIM + H_M
    cols = w_in.shape[-1]
    width = RWKV_COLS + SSM_COLS_PAD + FOX_COLS_PAD
    src = jnp.arange(cols, dtype=jnp.int32)
    dst = jnp.where(src < ssm_end, src, src - ssm_end + RWKV_COLS + SSM_COLS_PAD)
    place = (dst[:, None] == jnp.arange(width, dtype=jnp.int32)[None, :]).astype(BF16)
    return jnp.einsum("lkc,cd->lkd", w_in.astype(BF16), place, preferred_element_type=BF16)


def _inproj(h2d, g, w, tm):
    n, d = h2d.shape
    widths = (RWKV_COLS, SSM_COLS_PAD, FOX_COLS_PAD)
    assert w.shape[1] == sum(widths)
    return pl.pallas_call(
        _inproj_kernel,
        grid=(n // tm,),
        in_specs=[pl.BlockSpec((tm, d), lambda i: (i, 0)), _const_spec(g.shape), _const_spec(w.shape)],
        out_specs=[pl.BlockSpec((tm, c), lambda i: (i, 0)) for c in widths],
        out_shape=[jax.ShapeDtypeStruct((n, c), F32) for c in widths],
        name="inproj",
        compiler_params=pltpu.CompilerParams(
            dimension_semantics=("parallel",), vmem_limit_bytes=VMEM_LIMIT),
    )(h2d, g, w)


def _rwkv_kernel(p_ref, mu_ref, w0_ref, w2_ref, a0_ref, a2_ref, g2_ref, kk_ref, ka_ref,
                 rk_ref, lnw_ref, lnb_ref, tri_ref, o_ref, s_ref, prev_ref):
    nb, L, C = p_ref.shape[0], RWKV_CHUNK, C_R
    nh = C // HEAD_DIM
    R = nb * L

    @pl.when(pl.program_id(1) == 0)
    def _():
        s_ref[...] = jnp.zeros_like(s_ref)
        prev_ref[...] = jnp.zeros_like(prev_ref)

    p = jnp.concatenate([p_ref[bi] for bi in range(nb)], axis=0)
    row = _iota((R, 1), 0)
    shifted = pltpu.roll(p, 1, 0)
    for bi in range(nb):
        shifted = jnp.where(row == bi * L, prev_ref[bi, 7:8, :], shifted)
        prev_ref[bi] = p_ref[bi, L - 8:L, :]
    xs = p + (shifted - p) * mu_ref[...]
    r, k, v, lo = xs[:, :C], xs[:, C:2 * C], xs[:, 2 * C:3 * C], xs[:, 3 * C:]

    seg = ((_iota((C, C), 0) // HEAD_DIM) == (_iota((C, C), 1) // HEAD_DIM))
    seg_ones = jnp.where(seg, 1.0, 0.0).astype(BF16)
    seg_mean = jnp.where(seg, 1.0 / HEAD_DIM, 0.0).astype(BF16)

    logw = w0_ref[...] + _mm(jnp.tanh(lo), w2_ref[...], 2, 2)
    lw = -math.exp(-0.5) * _sigmoid(logw)
    lr = _sigmoid(a0_ref[...] + _mm(lo, a2_ref[...]))
    gate = _mm(_sigmoid(lo), g2_ref[...])
    kk = k * kk_ref[...]
    kk = kk / jnp.maximum(jnp.sqrt(_mm(kk * kk, seg_ones)), 1e-12)
    k = k * (1.0 + (lr - 1.0) * ka_ref[...])
    a = -kk
    b = kk * lr

    cum = _mm(tri_ref[...], lw, 1, 3)
    cl = jnp.concatenate([jnp.broadcast_to(cum[(bi + 1) * L - 1:(bi + 1) * L, :], (L, C))
                          for bi in range(nb)], axis=0)
    e_neg = jnp.exp(-cum)
    e_end = jnp.exp(cl - cum)
    chunks = lambda x: [x[bi * L:(bi + 1) * L] for bi in range(nb)]
    At, Rt, Kt, Bt = chunks(a * jnp.exp(cum - lw)), chunks(r * jnp.exp(cum)), chunks(k * e_neg), chunks(b * e_neg)
    Bp, Kp, vs = chunks(b * e_end), chunks(k * e_end), chunks(v)
    e_tot = [x[L - 1:L, :] for x in chunks(jnp.exp(cl))]

    bd = functools.partial(_block_diag, nblk=nh, rblk=L, cblk=HEAD_DIM)
    rows = range(nb)
    lhs = [jnp.concatenate([At[i], Rt[i]], axis=0) for i in rows]
    m1 = [_mm(lhs[i], jnp.concatenate([bd(Bt[i]), bd(Kt[i])], axis=0), dims=_NT) for i in rows]
    tcol = _iota((L, C), 1) % HEAD_DIM
    trow = _iota((L, C), 0)
    strict = tcol < trow
    incl = tcol <= trow
    a_ak = [jnp.where(strict, m[:L, C:], 0.0) for m in m1]
    r_rb = [jnp.where(incl, m[L:, :C], 0.0) for m in m1]
    r_rk = [jnp.where(incl, m[L:, C:], 0.0) for m in m1]

    x = [jnp.where(strict, m[:L, :C], 0.0) for m in m1]
    eye = jnp.where(tcol == trow, 1.0, 0.0)
    t_inv = [eye + xi for xi in x]
    x = [_mm(xi, bd(xi)) for xi in x]
    for _ in range(int(math.log2(L)) - 2):
        both = [_mm(jnp.concatenate([ti, xi], axis=0), bd(xi)) for ti, xi in zip(t_inv, x)]
        t_inv = [ti + b[:L] for ti, b in zip(t_inv, both)]
        x = [b[L:] for b in both]
    t_inv = [ti + _mm(ti, bd(xi)) for ti, xi in zip(t_inv, x)]

    s0 = [s_ref[i] for i in rows]
    from_s = [_mm(lhs[i], s0[i], dims=_NT) for i in rows]
    from_v = [_mm(jnp.concatenate([a_ak[i], r_rk[i]], axis=0), bd(vs[i])) for i in rows]
    u = [_mm(t_inv[i], bd(from_s[i][:L] + from_v[i][:L])) for i in rows]
    y = [from_s[i][L:] + from_v[i][L:] + _mm(r_rb[i], bd(u[i])) for i in rows]
    upd = [_mm(jnp.concatenate([u[i], vs[i]], axis=0).T,
               jnp.concatenate([Bp[i], Kp[i]], axis=0)) for i in rows]
    for i in rows:
        s_ref[i] = s0[i] * e_tot[i] + jnp.where(seg, upd[i], 0.0)

    y = jnp.concatenate(y, axis=0)
    mean = _mm(y, seg_mean)
    d = y - mean
    var = _mm(d * d, seg_mean)
    yn = d * lax.rsqrt(var + RWKV_GN_EPS) * lnw_ref[...] + lnb_ref[...]
    bonus = _mm(r * k * rk_ref[...], seg_ones) * v
    out = (yn + bonus) * gate
    for i in rows:
        o_ref[i] = out[i * L:(i + 1) * L]


def _rwkv(p_r, prm):
    bsz, t, cols = p_r.shape
    L = RWKV_CHUNK
    nb = RWKV_ROWS if bsz % RWKV_ROWS == 0 else 1
    names = ("mu", "w0", "w2", "a0", "a2", "g2", "k_k", "k_a", "r_k", "ln_w", "ln_b")
    rr, cc = _iota((nb * L, nb * L), 0), _iota((nb * L, nb * L), 1)
    tri = jnp.where((rr >= cc) & (rr // L == cc // L), 1.0, 0.0).astype(BF16)
    params = [prm[nm] for nm in names] + [tri]
    return pl.pallas_call(
        _rwkv_kernel,
        grid=(bsz // nb, t // L),
        in_specs=[pl.BlockSpec((nb, L, cols), lambda b, c: (b, c, 0))]
                 + [_const_spec(x.shape) for x in params],
        out_specs=pl.BlockSpec((nb, L, C_R), lambda b, c: (b, c, 0)),
        out_shape=jax.ShapeDtypeStruct((bsz, t, C_R), F32),
        scratch_shapes=[pltpu.VMEM((nb, C_R, C_R), F32), pltpu.VMEM((nb, 8, cols), F32)],
        name="rwkv7",
        compiler_params=pltpu.CompilerParams(
            dimension_semantics=("parallel", "arbitrary"), vmem_limit_bytes=VMEM_LIMIT),
    )(p_r, *params)


def _ssd_kernel(p_ref, cw_ref, cb_ref, dtb_ref, a_ref, d_ref, nw_ref, o_ref, xe_ref, s_ref):
    rows = [_ssd_row(p_ref.at[b], cw_ref, cb_ref, dtb_ref, a_ref, d_ref, nw_ref, o_ref.at[b], xe_ref.at[b],
                     s_ref.at[b]) for b in range(p_ref.shape[0])]
    _lockstep(rows)


def _ssd_row(p_ref, cw_ref, cb_ref, dtb_ref, a_ref, d_ref, nw_ref, o_ref, xe_ref, s_ref):
    L, N = SSD_CHUNK, SSM_STATE
    hg = H_M // SSM_GROUPS
    gw = hg * HEAD_DIM

    @pl.when(pl.program_id(1) == 0)
    def _():
        s_ref[...] = jnp.zeros_like(s_ref)
        xe_ref[0:8, :] = jnp.zeros((8, CONV_DIM), F32)

    @pl.when(pl.program_id(1) > 0)
    def _():
        xe_ref[0:8, :] = xe_ref[L:L + 8, :]

    p = p_ref[...]
    z = p[:, :C_M]
    xbc = p[:, C_M:C_M + CONV_DIM]
    xe_ref[8:8 + L, :] = xbc
    cw = cw_ref[...]
    conv = cb_ref[...] + xbc * cw[CONV_K - 1:CONV_K, :]
    for s in range(1, CONV_K):
        conv = conv + xe_ref[8 - s:8 - s + L, :] * cw[CONV_K - 1 - s:CONV_K - s, :]
    act = conv * _sigmoid(conv)
    yield
    xs = act[:, :C_M]
    bm = act[:, C_M:C_M + SSM_GROUPS * N]
    cm = act[:, C_M + SSM_GROUPS * N:]

    dt = _softplus(p[:, C_M + CONV_DIM:] + dtb_ref[...])
    a = dt * a_ref[...]
    tri = jnp.where(_iota((L, L), 0) >= _iota((L, L), 1), 1.0, 0.0).astype(BF16)
    acum = _mm(tri, a, 1, 3)
    yield
    acum_cols = [jnp.broadcast_to(acum[:, h:h + 1], (L, LANE)) for h in range(H_M)]
    dt_cols = [jnp.broadcast_to(dt[:, h:h + 1], (L, LANE)) for h in range(H_M)]
    first = _iota((L, LANE), 1) < HEAD_DIM
    pairs = lambda cols: jnp.concatenate(
        [jnp.where(first, cols[h], cols[h + 1]) for h in range(0, H_M, 2)], axis=1)
    acum_x = pairs(acum_cols)
    dt_x = pairs(dt_cols)
    acum_row = acum.T
    atot_x = acum_x[L - 1:L, :]
    xdt = xs * dt_x
    x_end = xdt * jnp.exp(atot_x - acum_x)
    e_in = jnp.exp(acum_x)
    e_tot = jnp.exp(atot_x)
    causal = _iota((L, L), 0) >= _iota((L, L), 1)
    yield

    s_old = s_ref[...]
    ys = []
    for g in range(SSM_GROUPS):
        bg = bm[:, g * N:(g + 1) * N]
        cg = cm[:, g * N:(g + 1) * N]
        cb = _mm(cg, bg, 1, 1, _NT)
        yield
        lmats = []
        for e in range(hg):
            h = g * hg + e
            diff = acum_cols[h] - acum_row[h:h + 1, :]
            lmats.append(cb * jnp.exp(jnp.where(causal, diff, NEG_BIG)))
        gs = slice(g * gw, (g + 1) * gw)
        yield
        y_diag = _mm(jnp.concatenate(lmats, axis=1), _block_diag(xdt[:, gs], hg, L, HEAD_DIM))
        sg = s_old[:, gs]
        y_off = _mm(cg, sg) * e_in[:, gs]
        s_ref[:, gs] = sg * e_tot[:, gs] + _mm(bg.T, x_end[:, gs])
        ys.append(y_diag + y_off)
        yield
    y = jnp.concatenate(ys, axis=1) + d_ref[...] * xs
    y = y * (z * _sigmoid(z))
    yield
    outs = []
    for g in range(SSM_GROUPS):
        yg = y[:, g * gw:(g + 1) * gw]
        outs.append(yg * lax.rsqrt(jnp.mean(yg * yg, axis=-1, keepdims=True) + SSM_NORM_EPS))
    o_ref[...] = jnp.concatenate(outs, axis=1) * nw_ref[...]
    yield


def _ssd(p_m, prm):
    bsz, t, cols = p_m.shape
    L = SSD_CHUNK
    nb = SSD_ROWS if bsz % SSD_ROWS == 0 else 1
    names = ("conv_w", "conv_b", "dt_bias", "a_neg", "d_x", "norm_w")
    params = [prm[nm] for nm in names]
    return pl.pallas_call(
        _ssd_kernel,
        grid=(bsz // nb, t // L),
        in_specs=[pl.BlockSpec((nb, L, cols), lambda b, c: (b, c, 0))]
                 + [_const_spec(x.shape) for x in params],
        out_specs=pl.BlockSpec((nb, L, C_M), lambda b, c: (b, c, 0)),
        out_shape=jax.ShapeDtypeStruct((bsz, t, C_M), F32),
        scratch_shapes=[pltpu.VMEM((nb, L + 8, CONV_DIM), F32), pltpu.VMEM((nb, SSM_STATE, C_M), F32)],
        name="ssd",
        compiler_params=pltpu.CompilerParams(
            dimension_semantics=("parallel", "arbitrary"), vmem_limit_bytes=VMEM_LIMIT),
    )(p_m, *params)


def _fox_prep_kernel(f_ref, fb_ref, nck_ref, carry_ref):
    nb, L = f_ref.shape[0], f_ref.shape[1]

    @pl.when(pl.program_id(1) == 0)
    def _():
        carry_ref[...] = jnp.zeros_like(carry_ref)

    tri = jnp.where(_iota((L, L), 0) >= _iota((L, L), 1), 1.0, 0.0).astype(BF16)
    for b in range(nb):
        log_f = -_softplus(-(f_ref[b] + fb_ref[...]))
        c = _mm(tri, log_f, 1, 3) + carry_ref[b, 0:1, :]
        carry_ref[b] = jnp.broadcast_to(c[L - 1:L, :], carry_ref.shape[1:])
        nck_ref[b] = (c * (-LOG2E)).T[0:8, :]


def _fox_prep(p_f, f_bias):
    bsz, t, cols = p_f.shape
    L = FOX_BLOCK
    nb = FOX_PREP_ROWS if bsz % FOX_PREP_ROWS == 0 else 1
    fblk = (cols - LANE) // LANE
    return pl.pallas_call(
        _fox_prep_kernel,
        grid=(bsz // nb, t // L),
        in_specs=[pl.BlockSpec((nb, L, LANE), lambda b, c: (b, c, fblk)), _const_spec(f_bias.shape)],
        out_specs=pl.BlockSpec((nb, 8, L), lambda b, c: (b, 0, c)),
        out_shape=jax.ShapeDtypeStruct((bsz, 8, t), F32),
        scratch_shapes=[pltpu.VMEM((nb, 8, LANE), F32)],
        name="fox_prep",
        compiler_params=pltpu.CompilerParams(
            dimension_semantics=("parallel", "arbitrary"), vmem_limit_bytes=VMEM_LIMIT),
    )(p_f, f_bias)


def _lockstep(gens, offset=0):
    def delayed(g, n):
        for _ in range(n):
            yield
        yield from g
    for _ in itertools.zip_longest(*[delayed(g, b * offset) for b, g in enumerate(gens)]):
        pass


def _fox_kernel(qi_ref, kj_ref, q_ref, k_ref, v_ref, nck_ref, nw_ref, o_ref, q2_ref, m_ref, l_ref, al_ref,
                p_ref, acc_ref):
    step_id = pl.program_id(1)
    i, j = qi_ref[step_id], kj_ref[step_id]
    nb, tq, tk = q_ref.shape[0], q_ref.shape[1], k_ref.shape[1]
    C = C_F
    zero = jnp.zeros((), BF16)

    def per_head_lanes(xs):
        first = _iota(xs[0].shape, 1) < HEAD_DIM
        return jnp.concatenate([jnp.where(first, xs[0], xs[1]), jnp.where(first, xs[2], xs[3])], axis=1)

    @pl.when(j == 0)
    def _():
        head = _iota((tq, C), 1) // HEAD_DIM
        for b in range(nb):
            q2 = (q_ref[b] * (LOG2E / math.sqrt(HEAD_DIM))).astype(BF16)
            for h in range(H_F):
                q2_ref[b, h] = jnp.where(head == h, q2, zero)
        m_ref[...] = jnp.full(m_ref.shape, NEG_BIG, F32)
        l_ref[...] = jnp.zeros_like(l_ref)
        acc_ref[...] = jnp.zeros_like(acc_ref)

    def step(b, masked):
        kb = k_ref[b].astype(BF16)
        vb = v_ref[b].astype(BF16)
        head = _iota((tk, C), 1) // HEAD_DIM
        nck = nck_ref[b]
        rs = FOX_STRIP
        for h in range(H_F):
            s_all = lax.dot_general(q2_ref[b, h], kb, _NT, preferred_element_type=F32)
            yield
            nk = nck[h:h + 1, :]
            for r in range(0, tq, rs):
                sh = s_all[r:r + rs, :] + nk
                if masked:
                    sh = jnp.where(r + _iota((rs, tk), 0) >= _iota((rs, tk), 1), sh, NEG_BIG)
                m_old = m_ref[b, h, r:r + rs, :]
                m_new = jnp.maximum(m_old, jnp.max(sh, axis=-1, keepdims=True))
                m_ref[b, h, r:r + rs, :] = m_new
                alpha = jnp.exp2(m_old - m_new)
                p = jnp.exp2(sh - jnp.concatenate([m_new] * (tk // LANE), axis=1))
                l_ref[b, h, r:r + rs, :] = (alpha * l_ref[b, h, r:r + rs, :]
                                            + jnp.sum(p, axis=-1, keepdims=True))
                al_ref[b, h, r:r + rs, :] = alpha
                p_ref[b, r:r + rs, h * tk:(h + 1) * tk] = p.astype(BF16)
            yield
            if h % 2 == 1:
                lanes = slice((h - 1) * HEAD_DIM, (h + 1) * HEAD_DIM)
                vpair = jnp.concatenate([jnp.where(head == g, vb, zero)[:, lanes] for g in (h - 1, h)], axis=0)
                first = _iota((tq, LANE), 1) < HEAD_DIM
                alpha = jnp.where(first, al_ref[b, h - 1], al_ref[b, h])
                acc_ref[b, :, lanes] = acc_ref[b, :, lanes] * alpha + jnp.dot(
                    p_ref[b, :, (h - 1) * tk:(h + 1) * tk], vpair, preferred_element_type=F32)
                yield

    @pl.when(j < i)
    def _():
        _lockstep([step(b, False) for b in range(nb)], offset=1)

    @pl.when(j == i)
    def _():
        _lockstep([step(b, True) for b in range(nb)], offset=1)
        seg = ((_iota((C, C), 0) // HEAD_DIM) == (_iota((C, C), 1) // HEAD_DIM))
        seg_mean = jnp.where(seg, 1.0 / HEAD_DIM, 0.0).astype(BF16)
        for b in range(nb):
            y = acc_ref[b] / per_head_lanes([l_ref[b, h] for h in range(H_F)])
            ms = _mm(y * y, seg_mean, 3, 1)
            o_ref[b] = y * lax.rsqrt(ms + NORM_EPS) * nw_ref[...]


def _fox(p_f, nck, norm_w):
    bsz, t, _ = p_f.shape
    blk = FOX_BLOCK
    nq = t // blk
    nb = FOX_ROWS if bsz % FOX_ROWS == 0 else 1
    pairs = [(i, j) for i in range(nq) for j in range(i + 1)]
    qi = jnp.asarray([p[0] for p in pairs], jnp.int32)
    kj = jnp.asarray([p[1] for p in pairs], jnp.int32)
    kv = lambda col: (lambda b, s, qi, kj: (b, kj[s], col))
    grid_spec = pltpu.PrefetchScalarGridSpec(
        num_scalar_prefetch=2,
        grid=(bsz // nb, len(pairs)),
        in_specs=[pl.BlockSpec((nb, blk, C_F), lambda b, s, qi, kj: (b, qi[s], 0)),
                  pl.BlockSpec((nb, blk, C_F), kv(1)),
                  pl.BlockSpec((nb, blk, C_F), kv(2)),
                  pl.BlockSpec((nb, 8, blk), lambda b, s, qi, kj: (b, 0, kj[s])),
                  pl.BlockSpec(norm_w.shape, lambda b, s, qi, kj: (0, 0))],
        out_specs=pl.BlockSpec((nb, blk, C_F), lambda b, s, qi, kj: (b, qi[s], 0)),
        scratch_shapes=[pltpu.VMEM((nb, H_F, blk, C_F), BF16), pltpu.VMEM((nb, H_F, blk, LANE), F32),
                        pltpu.VMEM((nb, H_F, blk, LANE), F32), pltpu.VMEM((nb, H_F, blk, LANE), F32),
                        pltpu.VMEM((nb, blk, H_F * blk), BF16), pltpu.VMEM((nb, blk, C_F), F32)])
    return pl.pallas_call(
        _fox_kernel,
        grid_spec=grid_spec,
        out_shape=jax.ShapeDtypeStruct((bsz, t, C_F), F32),
        name="fox_attn",
        compiler_params=pltpu.CompilerParams(
            dimension_semantics=("parallel", "arbitrary"), vmem_limit_bytes=VMEM_LIMIT),
    )(qi, kj, p_f, p_f, p_f, nck, norm_w)


def _rms(x, g):
    return x * lax.rsqrt(jnp.mean(x * x, axis=-1, keepdims=True) + NORM_EPS) * g


def _mix_mlp_kernel(h_ref, yr_ref, ym_ref, yf_ref, wor_ref, wom_ref, wof_ref, g1_ref, g2_ref, g3_ref,
                    wu_ref, wd_ref, o_ref):
    tm = h_ref.shape[0]
    halves = [slice(0, tm // 2), slice(tm // 2, tm)]
    dot = functools.partial(jnp.dot, preferred_element_type=F32)
    mix = [dot(yr_ref[r, :].astype(BF16), wor_ref[...]) + dot(ym_ref[r, :].astype(BF16), wom_ref[...])
           + dot(yf_ref[r, :].astype(BF16), wof_ref[...]) for r in halves]
    h1 = [h_ref[r, :] + _rms(m, g1_ref[...]) for r, m in zip(halves, mix)]
    u = [_rms(x, g2_ref[...]).astype(BF16) for x in h1]
    d_ff = wu_ref.shape[1]
    step = 1024
    ff = [None] * len(halves)
    for c0 in range(0, d_ff, step):
        hid = [dot(ui, wu_ref[:, c0:c0 + step]) for ui in u]
        hid = [jnp.square(jnp.maximum(x, 0.0)).astype(BF16) for x in hid]
        part = [dot(x, wd_ref[c0:c0 + step, :]) for x in hid]
        ff = [p if f is None else f + p for f, p in zip(ff, part)]
    for r, x, f in zip(halves, h1, ff):
        o_ref[r, :] = x + _rms(f, g3_ref[...])


def _mix_mlp(h2d, yr, ym, yf, prm, tm):
    n, d = h2d.shape
    names = ("wo_r", "wo_m", "wo_f", "g_mix_post", "g_mlp_pre", "g_mlp_post", "w_up", "w_down")
    params = [prm[nm] for nm in names]
    row = lambda w: pl.BlockSpec((tm, w), lambda i: (i, 0))
    return pl.pallas_call(
        _mix_mlp_kernel,
        grid=(n // tm,),
        in_specs=[row(d), row(yr.shape[1]), row(ym.shape[1]), row(yf.shape[1])]
                 + [_const_spec(x.shape) for x in params],
        out_specs=row(d),
        out_shape=jax.ShapeDtypeStruct((n, d), F32),
        name="mix_mlp",
        compiler_params=pltpu.CompilerParams(
            dimension_semantics=("parallel",), vmem_limit_bytes=VMEM_LIMIT),
    )(h2d, yr, ym, yf, *params)


def _pad_cols(w, width):
    return jnp.pad(w, ((0, 0), (0, width - w.shape[1])))


def _layer_params(l, norm_mix_pre, norm_mix_post, w_in, rwkv_mu, rwkv_w0, rwkv_w2, rwkv_a0, rwkv_a2,
                  rwkv_g2, rwkv_k_k, rwkv_k_a, rwkv_r_k, rwkv_ln_w, rwkv_ln_b, ssm_conv_w, ssm_conv_b,
                  ssm_dt_bias, ssm_A_log, ssm_D, ssm_norm_w, fox_f_bias, fox_norm_w, w_out,
                  norm_mlp_pre, norm_mlp_post, w_mlp_up, w_mlp_down):
    row = lambda x: x.reshape(1, -1).astype(F32)
    lo_rows = R_DECAY + R_AAA + R_GATE
    place = lambda m, r0: jnp.pad(m, ((r0, lo_rows - r0 - m.shape[0]), (0, 0)))
    inproj = dict(g=row(norm_mix_pre[l]))
    rwkv = dict(mu=row(rwkv_mu[l]), w0=row(rwkv_w0[l]), w2=place(rwkv_w2[l], 0), a0=row(rwkv_a0[l]),
                a2=place(rwkv_a2[l], R_DECAY), g2=place(rwkv_g2[l], R_DECAY + R_AAA),
                k_k=row(rwkv_k_k[l]), k_a=row(rwkv_k_a[l]), r_k=row(rwkv_r_k[l]),
                ln_w=row(rwkv_ln_w[l]), ln_b=row(rwkv_ln_b[l]))
    ssd = dict(conv_w=ssm_conv_w[l], conv_b=row(ssm_conv_b[l]),
               dt_bias=_pad_cols(row(ssm_dt_bias[l]), LANE),
               a_neg=_pad_cols(row(-jnp.exp(ssm_A_log[l].astype(F32))), LANE),
               d_x=row(jnp.repeat(ssm_D[l], HEAD_DIM)), norm_w=row(ssm_norm_w[l]))
    fox = dict(f_bias=_pad_cols(row(fox_f_bias[l]), LANE), norm_w=row(fox_norm_w[l]))
    wo = w_out[l].astype(BF16)
    mlp = dict(wo_r=wo[:C_R], wo_m=wo[C_R:C_R + C_M], wo_f=wo[C_R + C_M:],
               g_mix_post=row(norm_mix_post[l]), g_mlp_pre=row(norm_mlp_pre[l]),
               g_mlp_post=row(norm_mlp_post[l]), w_up=w_mlp_up[l].astype(BF16),
               w_down=w_mlp_down[l].astype(BF16))
    return inproj, rwkv, ssd, fox, mlp


def kernel(x, norm_mix_pre, norm_mix_post, w_in, rwkv_mu, rwkv_w0, rwkv_w2, rwkv_a0, rwkv_a2, rwkv_g2, rwkv_k_k, rwkv_k_a, rwkv_r_k, rwkv_ln_w, rwkv_ln_b, ssm_conv_w, ssm_conv_b, ssm_dt_bias, ssm_A_log, ssm_D, ssm_norm_w, fox_f_bias, fox_norm_w, w_out, norm_mlp_pre, norm_mlp_post, w_mlp_up, w_mlp_down):
    weights = (norm_mix_pre, norm_mix_post, w_in, rwkv_mu, rwkv_w0, rwkv_w2, rwkv_a0, rwkv_a2, rwkv_g2,
               rwkv_k_k, rwkv_k_a, rwkv_r_k, rwkv_ln_w, rwkv_ln_b, ssm_conv_w, ssm_conv_b, ssm_dt_bias,
               ssm_A_log, ssm_D, ssm_norm_w, fox_f_bias, fox_norm_w, w_out, norm_mlp_pre, norm_mlp_post,
               w_mlp_up, w_mlp_down)
    bsz, t, d = x.shape
    n = bsz * t
    assert t % FOX_BLOCK == 0 and t % SSD_CHUNK == 0 and t % RWKV_CHUNK == 0
    tm = 512 if n % 512 == 0 else 256
    h = x.reshape(n, d)
    w_proj = _inproj_weights(w_in)
    for l in range(w_in.shape[0]):
        inproj, rwkv, ssd, fox, mlp = _layer_params(l, *weights)
        p_r, p_m, p_f = _inproj(h, inproj["g"], w_proj[l], tm)
        y_r = _rwkv(p_r.reshape(bsz, t, -1), rwkv)
        y_m = _ssd(p_m.reshape(bsz, t, -1), ssd)
        p_f3 = p_f.reshape(bsz, t, -1)
        nck = _fox_prep(p_f3, fox["f_bias"])
        y_f = _fox(p_f3, nck, fox["norm_w"])
        h = _mix_mlp(h, y_r.reshape(n, -1), y_m.reshape(n, -1), y_f.reshape(n, -1), mlp, tm)
    return h.reshape(bsz, t, d)
```

```python
import functools
import itertools
import math

import jax
import jax.numpy as jnp
from jax import lax
from jax.experimental import pallas as pl
from jax.experimental.pallas import tpu as pltpu

F32 = jnp.float32
BF16 = jnp.bfloat16

HEAD_DIM = 64
NORM_EPS = 1e-6
C_R = 256
R_DECAY, R_AAA, R_GATE = 32, 32, 64
RWKV_GN_EPS = 64e-5
RWKV_COLS = 3 * C_R + R_DECAY + R_AAA + R_GATE
RWKV_CHUNK = 64
RWKV_ROWS = 8
C_M = 512
H_M = 8
SSM_STATE = 128
SSM_GROUPS = 2
CONV_K = 4
CONV_DIM = C_M + 2 * SSM_GROUPS * SSM_STATE
SSD_CHUNK = 128
SSD_ROWS = 8
SSM_NORM_EPS = 1e-5
SSM_COLS_PAD = C_M + CONV_DIM + 128
C_F = 256
H_F = 4
FOX_COLS_PAD = 3 * C_F + 128
FOX_BLOCK = 512
FOX_STRIP = 32
FOX_ROWS = 2
FOX_PREP_ROWS = 8
LOG2E = 1.4426950408889634
LANE = 128
NEG_BIG = -1e30

VMEM_LIMIT = 48 * 1024 * 1024

_NN = (((1,), (0,)), ((), ()))
_NT = (((1,), (1,)), ((), ()))


def _split(x, n):
    if x.dtype == BF16:
        return [x]
    terms = []
    rem = x
    for i in range(n):
        t = rem.astype(BF16)
        terms.append(t)
        if i + 1 < n:
            rem = rem - t.astype(F32)
    return terms


def _mm(a, b, na=1, nb=1, dims=_NN):
    ta, tb = _split(a, na), _split(b, nb)
    depth = max(len(ta), len(tb))
    out = None
    for i, x in enumerate(ta):
        for j, y in enumerate(tb):
            if i + j < depth:
                d = lax.dot_general(x, y, dims, preferred_element_type=F32)
                out = d if out is None else out + d
    return out


def _sigmoid(x):
    return 1.0 / (1.0 + jnp.exp(-x))


def _softplus(x):
    return jnp.maximum(x, 0.0) + jnp.log1p(jnp.exp(-jnp.abs(x)))


def _iota(shape, dim):
    return lax.broadcasted_iota(jnp.int32, shape, dim)


def _block_diag(x, nblk, rblk, cblk):
    t = jnp.concatenate([x.astype(BF16)] * nblk, axis=0)
    shape = t.shape
    keep = (_iota(shape, 0) // rblk) == (_iota(shape, 1) // cblk)
    return jnp.where(keep, t, jnp.zeros((), t.dtype))


def _inproj_kernel(h_ref, g_ref, w_ref, pr_ref, pm_ref, pf_ref):
    x = h_ref[...]
    ms = jnp.mean(x * x, axis=-1, keepdims=True)
    u = (x * lax.rsqrt(ms + NORM_EPS) * g_ref[...]).astype(BF16)
    proj = jnp.dot(u, w_ref[...], preferred_element_type=F32)
    c0 = 0
    for o_ref in (pr_ref, pm_ref, pf_ref):
        c1 = c0 + o_ref.shape[1]
        o_ref[...] = proj[:, c0:c1]
        c0 = c1


def _const_spec(shape):
    nd = len(shape)
    return pl.BlockSpec(shape, lambda *_: (0,) * nd, pipeline_mode=pl.Buffered(1))


def _inproj_weights(w_in):
    ssm_end = RWKV_COLS + C_M + CONV_DIM + H_M
    cols = w_in.shape[-1]
    width = RWKV_COLS + SSM_COLS_PAD + FOX_COLS_PAD
    src = jnp.arange(cols, dtype=jnp.int32)
    dst = jnp.where(src < ssm_end, src, src - ssm_end + RWKV_COLS + SSM_COLS_PAD)
    place = (dst[:, None] == jnp.arange(width, dtype=jnp.int32)[None, :]).astype(BF16)
    return jnp.einsum("lkc,cd->lkd", w_in.astype(BF16), place, preferred_element_type=BF16)


def _inproj(h2d, g, w, tm):
    n, d = h2d.shape
    widths = (RWKV_COLS, SSM_COLS_PAD, FOX_COLS_PAD)
    assert w.shape[1] == sum(widths)
    return pl.pallas_call(
        _inproj_kernel,
        grid=(n // tm,),
        in_specs=[pl.BlockSpec((tm, d), lambda i: (i, 0)), _const_spec(g.shape), _const_spec(w.shape)],
        out_specs=[pl.BlockSpec((tm, c), lambda i: (i, 0)) for c in widths],
        out_shape=[jax.ShapeDtypeStruct((n, c), F32) for c in widths],
        name="inproj",
        compiler_params=pltpu.CompilerParams(
            dimension_semantics=("parallel",), vmem_limit_bytes=VMEM_LIMIT),
    )(h2d, g, w)


def _rwkv_kernel(p_ref, mu_ref, w0_ref, w2_ref, a0_ref, a2_ref, g2_ref, kk_ref, ka_ref,
                 rk_ref, lnw_ref, lnb_ref, tri_ref, o_ref, s_ref, prev_ref):
    nb, L, C = p_ref.shape[0], RWKV_CHUNK, C_R
    nh = C // HEAD_DIM
    R = nb * L

    @pl.when(pl.program_id(1) == 0)
    def _():
        s_ref[...] = jnp.zeros_like(s_ref)
        prev_ref[...] = jnp.zeros_like(prev_ref)

    p = jnp.concatenate([p_ref[bi] for bi in range(nb)], axis=0)
    row = _iota((R, 1), 0)
    shifted = pltpu.roll(p, 1, 0)
    for bi in range(nb):
        shifted = jnp.where(row == bi * L, prev_ref[bi, 7:8, :], shifted)
        prev_ref[bi] = p_ref[bi, L - 8:L, :]
    xs = p + (shifted - p) * mu_ref[...]
    r, k, v, lo = xs[:, :C], xs[:, C:2 * C], xs[:, 2 * C:3 * C], xs[:, 3 * C:]

    seg = ((_iota((C, C), 0) // HEAD_DIM) == (_iota((C, C), 1) // HEAD_DIM))
    seg_ones = jnp.where(seg, 1.0, 0.0).astype(BF16)
    seg_mean = jnp.where(seg, 1.0 / HEAD_DIM, 0.0).astype(BF16)

    logw = w0_ref[...] + _mm(jnp.tanh(lo), w2_ref[...], 2, 2)
    lw = -math.exp(-0.5) * _sigmoid(logw)
    lr = _sigmoid(a0_ref[...] + _mm(lo, a2_ref[...]))
    gate = _mm(_sigmoid(lo), g2_ref[...])
    kk = k * kk_ref[...]
    kk = kk / jnp.maximum(jnp.sqrt(_mm(kk * kk, seg_ones)), 1e-12)
    k = k * (1.0 + (lr - 1.0) * ka_ref[...])
    a = -kk
    b = kk * lr

    cum = _mm(tri_ref[...], lw, 1, 3)
    cl = jnp.concatenate([jnp.broadcast_to(cum[(bi + 1) * L - 1:(bi + 1) * L, :], (L, C))
                          for bi in range(nb)], axis=0)
    e_neg = jnp.exp(-cum)
    e_end = jnp.exp(cl - cum)
    chunks = lambda x: [x[bi * L:(bi + 1) * L] for bi in range(nb)]
    At, Rt, Kt, Bt = chunks(a * jnp.exp(cum - lw)), chunks(r * jnp.exp(cum)), chunks(k * e_neg), chunks(b * e_neg)
    Bp, Kp, vs = chunks(b * e_end), chunks(k * e_end), chunks(v)
    e_tot = [x[L - 1:L, :] for x in chunks(jnp.exp(cl))]

    bd = functools.partial(_block_diag, nblk=nh, rblk=L, cblk=HEAD_DIM)
    rows = range(nb)
    lhs = [jnp.concatenate([At[i], Rt[i]], axis=0) for i in rows]
    m1 = [_mm(lhs[i], jnp.concatenate([bd(Bt[i]), bd(Kt[i])], axis=0), dims=_NT) for i in rows]
    tcol = _iota((L, C), 1) % HEAD_DIM
    trow = _iota((L, C), 0)
    strict = tcol < trow
    incl = tcol <= trow
    a_ak = [jnp.where(strict, m[:L, C:], 0.0) for m in m1]
    r_rb = [jnp.where(incl, m[L:, :C], 0.0) for m in m1]
    r_rk = [jnp.where(incl, m[L:, C:], 0.0) for m in m1]

    x = [jnp.where(strict, m[:L, :C], 0.0) for m in m1]
    eye = jnp.where(tcol == trow, 1.0, 0.0)
    t_inv = [eye + xi for xi in x]
    x = [_mm(xi, bd(xi)) for xi in x]
    for _ in range(int(math.log2(L)) - 2):
        both = [_mm(jnp.concatenate([ti, xi], axis=0), bd(xi)) for ti, xi in zip(t_inv, x)]
        t_inv = [ti + b[:L] for ti, b in zip(t_inv, both)]
        x = [b[L:] for b in both]
    t_inv = [ti + _mm(ti, bd(xi)) for ti, xi in zip(t_inv, x)]

    s0 = [s_ref[i] for i in rows]
    from_s = [_mm(lhs[i], s0[i], dims=_NT) for i in rows]
    from_v = [_mm(jnp.concatenate([a_ak[i], r_rk[i]], axis=0), bd(vs[i])) for i in rows]
    u = [_mm(t_inv[i], bd(from_s[i][:L] + from_v[i][:L])) for i in rows]
    y = [from_s[i][L:] + from_v[i][L:] + _mm(r_rb[i], bd(u[i])) for i in rows]
    upd = [_mm(jnp.concatenate([u[i], vs[i]], axis=0).T,
               jnp.concatenate([Bp[i], Kp[i]], axis=0)) for i in rows]
    for i in rows:
        s_ref[i] = s0[i] * e_tot[i] + jnp.where(seg, upd[i], 0.0)

    y = jnp.concatenate(y, axis=0)
    mean = _mm(y, seg_mean)
    d = y - mean
    var = _mm(d * d, seg_mean)
    yn = d * lax.rsqrt(var + RWKV_GN_EPS) * lnw_ref[...] + lnb_ref[...]
    bonus = _mm(r * k * rk_ref[...], seg_ones) * v
    out = (yn + bonus) * gate
    for i in rows:
        o_ref[i] = out[i * L:(i + 1) * L]


def _rwkv(p_r, prm):
    bsz, t, cols = p_r.shape
    L = RWKV_CHUNK
    nb = RWKV_ROWS if bsz % RWKV_ROWS == 0 else 1
    names = ("mu", "w0", "w2", "a0", "a2", "g2", "k_k", "k_a", "r_k", "ln_w", "ln_b")
    rr, cc = _iota((nb * L, nb * L), 0), _iota((nb * L, nb * L), 1)
    tri = jnp.where((rr >= cc) & (rr // L == cc // L), 1.0, 0.0).astype(BF16)
    params = [prm[nm] for nm in names] + [tri]
    return pl.pallas_call(
        _rwkv_kernel,
        grid=(bsz // nb, t // L),
        in_specs=[pl.BlockSpec((nb, L, cols), lambda b, c: (b, c, 0))]
                 + [_const_spec(x.shape) for x in params],
        out_specs=pl.BlockSpec((nb, L, C_R), lambda b, c: (b, c, 0)),
        out_shape=jax.ShapeDtypeStruct((bsz, t, C_R), F32),
        scratch_shapes=[pltpu.VMEM((nb, C_R, C_R), F32), pltpu.VMEM((nb, 8, cols), F32)],
        name="rwkv7",
        compiler_params=pltpu.CompilerParams(
            dimension_semantics=("parallel", "arbitrary"), vmem_limit_bytes=VMEM_LIMIT),
    )(p_r, *params)


def _ssd_kernel(p_ref, cw_ref, cb_ref, dtb_ref, a_ref, d_ref, nw_ref, o_ref, xe_ref, s_ref):
    rows = [_ssd_row(p_ref.at[b], cw_ref, cb_ref, dtb_ref, a_ref, d_ref, nw_ref, o_ref.at[b], xe_ref.at[b],
                     s_ref.at[b]) for b in range(p_ref.shape[0])]
    _lockstep(rows)


def _ssd_row(p_ref, cw_ref, cb_ref, dtb_ref, a_ref, d_ref, nw_ref, o_ref, xe_ref, s_ref):
    L, N = SSD_CHUNK, SSM_STATE
    hg = H_M // SSM_GROUPS
    gw = hg * HEAD_DIM

    @pl.when(pl.program_id(1) == 0)
    def _():
        s_ref[...] = jnp.zeros_like(s_ref)
        xe_ref[0:8, :] = jnp.zeros((8, CONV_DIM), F32)

    @pl.when(pl.program_id(1) > 0)
    def _():
        xe_ref[0:8, :] = xe_ref[L:L + 8, :]

    p = p_ref[...]
    z = p[:, :C_M]
    xbc = p[:, C_M:C_M + CONV_DIM]
    xe_ref[8:8 + L, :] = xbc
    cw = cw_ref[...]
    conv = cb_ref[...] + xbc * cw[CONV_K - 1:CONV_K, :]
    for s in range(1, CONV_K):
        conv = conv + xe_ref[8 - s:8 - s + L, :] * cw[CONV_K - 1 - s:CONV_K - s, :]
    act = conv * _sigmoid(conv)
    yield
    xs = act[:, :C_M]
    bm = act[:, C_M:C_M + SSM_GROUPS * N]
    cm = act[:, C_M + SSM_GROUPS * N:]

    dt = _softplus(p[:, C_M + CONV_DIM:] + dtb_ref[...])
    a = dt * a_ref[...]
    tri = jnp.where(_iota((L, L), 0) >= _iota((L, L), 1), 1.0, 0.0).astype(BF16)
    acum = _mm(tri, a, 1, 3)
    yield
    acum_cols = [jnp.broadcast_to(acum[:, h:h + 1], (L, LANE)) for h in range(H_M)]
    dt_cols = [jnp.broadcast_to(dt[:, h:h + 1], (L, LANE)) for h in range(H_M)]
    first = _iota((L, LANE), 1) < HEAD_DIM
    pairs = lambda cols: jnp.concatenate(
        [jnp.where(first, cols[h], cols[h + 1]) for h in range(0, H_M, 2)], axis=1)
    acum_x = pairs(acum_cols)
    dt_x = pairs(dt_cols)
    acum_row = acum.T
    atot_x = acum_x[L - 1:L, :]
    xdt = xs * dt_x
    x_end = xdt * jnp.exp(atot_x - acum_x)
    e_in = jnp.exp(acum_x)
    e_tot = jnp.exp(atot_x)
    causal = _iota((L, L), 0) >= _iota((L, L), 1)
    yield

    s_old = s_ref[...]
    ys = []
    for g in range(SSM_GROUPS):
        bg = bm[:, g * N:(g + 1) * N]
        cg = cm[:, g * N:(g + 1) * N]
        cb = _mm(cg, bg, 1, 1, _NT)
        yield
        lmats = []
        for e in range(hg):
            h = g * hg + e
            diff = acum_cols[h] - acum_row[h:h + 1, :]
            lmats.append(cb * jnp.exp(jnp.where(causal, diff, NEG_BIG)))
        gs = slice(g * gw, (g + 1) * gw)
        yield
        y_diag = _mm(jnp.concatenate(lmats, axis=1), _block_diag(xdt[:, gs], hg, L, HEAD_DIM))
        sg = s_old[:, gs]
        y_off = _mm(cg, sg) * e_in[:, gs]
        s_ref[:, gs] = sg * e_tot[:, gs] + _mm(bg.T, x_end[:, gs])
        ys.append(y_diag + y_off)
        yield
    y = jnp.concatenate(ys, axis=1) + d_ref[...] * xs
    y = y * (z * _sigmoid(z))
    yield
    outs = []
    for g in range(SSM_GROUPS):
        yg = y[:, g * gw:(g + 1) * gw]
        outs.append(yg * lax.rsqrt(jnp.mean(yg * yg, axis=-1, keepdims=True) + SSM_NORM_EPS))
    o_ref[...] = jnp.concatenate(outs, axis=1) * nw_ref[...]
    yield


def _ssd(p_m, prm):
    bsz, t, cols = p_m.shape
    L = SSD_CHUNK
    nb = SSD_ROWS if bsz % SSD_ROWS == 0 else 1
    names = ("conv_w", "conv_b", "dt_bias", "a_neg", "d_x", "norm_w")
    params = [prm[nm] for nm in names]
    return pl.pallas_call(
        _ssd_kernel,
        grid=(bsz // nb, t // L),
        in_specs=[pl.BlockSpec((nb, L, cols), lambda b, c: (b, c, 0))]
                 + [_const_spec(x.shape) for x in params],
        out_specs=pl.BlockSpec((nb, L, C_M), lambda b, c: (b, c, 0)),
        out_shape=jax.ShapeDtypeStruct((bsz, t, C_M), F32),
        scratch_shapes=[pltpu.VMEM((nb, L + 8, CONV_DIM), F32), pltpu.VMEM((nb, SSM_STATE, C_M), F32)],
        name="ssd",
        compiler_params=pltpu.CompilerParams(
            dimension_semantics=("parallel", "arbitrary"), vmem_limit_bytes=VMEM_LIMIT),
    )(p_m, *params)


def _fox_prep_kernel(f_ref, fb_ref, nck_ref, carry_ref):
    nb, L = f_ref.shape[0], f_ref.shape[1]

    @pl.when(pl.program_id(1) == 0)
    def _():
        carry_ref[...] = jnp.zeros_like(carry_ref)

    tri = jnp.where(_iota((L, L), 0) >= _iota((L, L), 1), 1.0, 0.0).astype(BF16)
    log_f = jnp.concatenate([-_softplus(-(f_ref[b] + fb_ref[...])) for b in range(nb)], axis=1)
    carry = jnp.concatenate([carry_ref[b, 0:1, :] for b in range(nb)], axis=1)
    c_all = _mm(tri, log_f, 1, 3) + carry
    for b in range(nb):
        c = c_all[:, b * LANE:(b + 1) * LANE]
        carry_ref[b] = jnp.broadcast_to(c[L - 1:L, :], carry_ref.shape[1:])
        nck_ref[b] = (c * (-LOG2E)).T[0:8, :]


def _fox_prep(p_f, f_bias):
    bsz, t, cols = p_f.shape
    L = FOX_BLOCK
    nb = FOX_PREP_ROWS if bsz % FOX_PREP_ROWS == 0 else 1
    fblk = (cols - LANE) // LANE
    return pl.pallas_call(
        _fox_prep_kernel,
        grid=(bsz // nb, t // L),
        in_specs=[pl.BlockSpec((nb, L, LANE), lambda b, c: (b, c, fblk)), _const_spec(f_bias.shape)],
        out_specs=pl.BlockSpec((nb, 8, L), lambda b, c: (b, 0, c)),
        out_shape=jax.ShapeDtypeStruct((bsz, 8, t), F32),
        scratch_shapes=[pltpu.VMEM((nb, 8, LANE), F32)],
        name="fox_prep",
        compiler_params=pltpu.CompilerParams(
            dimension_semantics=("parallel", "arbitrary"), vmem_limit_bytes=VMEM_LIMIT),
    )(p_f, f_bias)


def _lockstep(gens, offset=0):
    def delayed(g, n):
        for _ in range(n):
            yield
        yield from g
    for _ in itertools.zip_longest(*[delayed(g, b * offset) for b, g in enumerate(gens)]):
        pass


def _fox_kernel(qi_ref, kj_ref, q_ref, k_ref, v_ref, nck_ref, nw_ref, o_ref, q2_ref, m_ref, l_ref, al_ref,
                p_ref, acc_ref):
    step_id = pl.program_id(1)
    i, j = qi_ref[step_id], kj_ref[step_id]
    nb, tq, tk = q_ref.shape[0], q_ref.shape[1], k_ref.shape[1]
    C = C_F
    zero = jnp.zeros((), BF16)

    def per_head_lanes(xs):
        first = _iota(xs[0].shape, 1) < HEAD_DIM
        return jnp.concatenate([jnp.where(first, xs[0], xs[1]), jnp.where(first, xs[2], xs[3])], axis=1)

    @pl.when(j == 0)
    def _():
        head = _iota((tq, C), 1) // HEAD_DIM
        for b in range(nb):
            q2 = (q_ref[b] * (LOG2E / math.sqrt(HEAD_DIM))).astype(BF16)
            for h in range(H_F):
                q2_ref[b, h] = jnp.where(head == h, q2, zero)
        m_ref[...] = jnp.full(m_ref.shape, NEG_BIG, F32)
        l_ref[...] = jnp.zeros_like(l_ref)
        acc_ref[...] = jnp.zeros_like(acc_ref)

    def step(b, masked):
        kb = k_ref[b].astype(BF16)
        vb = v_ref[b].astype(BF16)
        head = _iota((tk, C), 1) // HEAD_DIM
        nck = nck_ref[b]
        rs = FOX_STRIP
        for h in range(H_F):
            s_all = lax.dot_general(q2_ref[b, h], kb, _NT, preferred_element_type=F32)
            yield
            nk = nck[h:h + 1, :]
            for r in range(0, tq, rs):
                sh = s_all[r:r + rs, :] + nk
                if masked:
                    sh = jnp.where(r + _iota((rs, tk), 0) >= _iota((rs, tk), 1), sh, NEG_BIG)
                m_old = m_ref[b, h, r:r + rs, :]
                m_new = jnp.maximum(m_old, jnp.max(sh, axis=-1, keepdims=True))
                m_ref[b, h, r:r + rs, :] = m_new
                alpha = jnp.exp2(m_old - m_new)
                p = jnp.exp2(sh - jnp.concatenate([m_new] * (tk // LANE), axis=1))
                l_ref[b, h, r:r + rs, :] = (alpha * l_ref[b, h, r:r + rs, :]
                                            + jnp.sum(p, axis=-1, keepdims=True))
                al_ref[b, h, r:r + rs, :] = alpha
                p_ref[b, r:r + rs, h * tk:(h + 1) * tk] = p.astype(BF16)
            yield
            if h % 2 == 1:
                lanes = slice((h - 1) * HEAD_DIM, (h + 1) * HEAD_DIM)
                vpair = jnp.concatenate([jnp.where(head == g, vb, zero)[:, lanes] for g in (h - 1, h)], axis=0)
                first = _iota((tq, LANE), 1) < HEAD_DIM
                alpha = jnp.where(first, al_ref[b, h - 1], al_ref[b, h])
                acc_ref[b, :, lanes] = acc_ref[b, :, lanes] * alpha + jnp.dot(
                    p_ref[b, :, (h - 1) * tk:(h + 1) * tk], vpair, preferred_element_type=F32)
                yield

    @pl.when(j < i)
    def _():
        _lockstep([step(b, False) for b in range(nb)], offset=2)

    @pl.when(j == i)
    def _():
        _lockstep([step(b, True) for b in range(nb)], offset=2)
        seg = ((_iota((C, C), 0) // HEAD_DIM) == (_iota((C, C), 1) // HEAD_DIM))
        seg_mean = jnp.where(seg, 1.0 / HEAD_DIM, 0.0).astype(BF16)
        for b in range(nb):
            y = acc_ref[b] / per_head_lanes([l_ref[b, h] for h in range(H_F)])
            ms = _mm(y * y, seg_mean, 3, 1)
            o_ref[b] = y * lax.rsqrt(ms + NORM_EPS) * nw_ref[...]


def _fox(p_f, nck, norm_w):
    bsz, t, _ = p_f.shape
    blk = FOX_BLOCK
    nq = t // blk
    nb = FOX_ROWS if bsz % FOX_ROWS == 0 else 1
    pairs = [(i, j) for i in range(nq) for j in range(i + 1)]
    qi = jnp.asarray([p[0] for p in pairs], jnp.int32)
    kj = jnp.asarray([p[1] for p in pairs], jnp.int32)
    kv = lambda col: (lambda b, s, qi, kj: (b, kj[s], col))
    grid_spec = pltpu.PrefetchScalarGridSpec(
        num_scalar_prefetch=2,
        grid=(bsz // nb, len(pairs)),
        in_specs=[pl.BlockSpec((nb, blk, C_F), lambda b, s, qi, kj: (b, qi[s], 0)),
                  pl.BlockSpec((nb, blk, C_F), kv(1)),
                  pl.BlockSpec((nb, blk, C_F), kv(2)),
                  pl.BlockSpec((nb, 8, blk), lambda b, s, qi, kj: (b, 0, kj[s])),
                  pl.BlockSpec(norm_w.shape, lambda b, s, qi, kj: (0, 0))],
        out_specs=pl.BlockSpec((nb, blk, C_F), lambda b, s, qi, kj: (b, qi[s], 0)),
        scratch_shapes=[pltpu.VMEM((nb, H_F, blk, C_F), BF16), pltpu.VMEM((nb, H_F, blk, LANE), F32),
                        pltpu.VMEM((nb, H_F, blk, LANE), F32), pltpu.VMEM((nb, H_F, blk, LANE), F32),
                        pltpu.VMEM((nb, blk, H_F * blk), BF16), pltpu.VMEM((nb, blk, C_F), F32)])
    return pl.pallas_call(
        _fox_kernel,
        grid_spec=grid_spec,
        out_shape=jax.ShapeDtypeStruct((bsz, t, C_F), F32),
        name="fox_attn",
        compiler_params=pltpu.CompilerParams(
            dimension_semantics=("parallel", "arbitrary"), vmem_limit_bytes=VMEM_LIMIT),
    )(qi, kj, p_f, p_f, p_f, nck, norm_w)


def _rms(x, g):
    return x * lax.rsqrt(jnp.mean(x * x, axis=-1, keepdims=True) + NORM_EPS) * g


def _mix_mlp_kernel(h_ref, yr_ref, ym_ref, yf_ref, wor_ref, wom_ref, wof_ref, g1_ref, g2_ref, g3_ref,
                    wu_ref, wd_ref, o_ref):
    tm = h_ref.shape[0]
    halves = [slice(0, tm // 2), slice(tm // 2, tm)]
    dot = functools.partial(jnp.dot, preferred_element_type=F32)
    mix = [dot(yr_ref[r, :].astype(BF16), wor_ref[...]) + dot(ym_ref[r, :].astype(BF16), wom_ref[...])
           + dot(yf_ref[r, :].astype(BF16), wof_ref[...]) for r in halves]
    h1 = [h_ref[r, :] + _rms(m, g1_ref[...]) for r, m in zip(halves, mix)]
    u = [_rms(x, g2_ref[...]).astype(BF16) for x in h1]
    d_ff = wu_ref.shape[1]
    step = 1024
    ff = [None] * len(halves)
    for c0 in range(0, d_ff, step):
        hid = [dot(ui, wu_ref[:, c0:c0 + step]) for ui in u]
        hid = [jnp.square(jnp.maximum(x, 0.0)).astype(BF16) for x in hid]
        part = [dot(x, wd_ref[c0:c0 + step, :]) for x in hid]
        ff = [p if f is None else f + p for f, p in zip(ff, part)]
    for r, x, f in zip(halves, h1, ff):
        o_ref[r, :] = x + _rms(f, g3_ref[...])


def _mix_mlp(h2d, yr, ym, yf, prm, tm):
    n, d = h2d.shape
    names = ("wo_r", "wo_m", "wo_f", "g_mix_post", "g_mlp_pre", "g_mlp_post", "w_up", "w_down")
    params = [prm[nm] for nm in names]
    row = lambda w: pl.BlockSpec((tm, w), lambda i: (i, 0))
    return pl.pallas_call(
        _mix_mlp_kernel,
        grid=(n // tm,),
        in_specs=[row(d), row(yr.shape[1]), row(ym.shape[1]), row(yf.shape[1])]
                 + [_const_spec(x.shape) for x in params],
        out_specs=row(d),
        out_shape=jax.ShapeDtypeStruct((n, d), F32),
        name="mix_mlp",
        compiler_params=pltpu.CompilerParams(
            dimension_semantics=("parallel",), vmem_limit_bytes=VMEM_LIMIT),
    )(h2d, yr, ym, yf, *params)


def _pad_cols(w, width):
    return jnp.pad(w, ((0, 0), (0, width - w.shape[1])))


def _layer_params(l, norm_mix_pre, norm_mix_post, w_in, rwkv_mu, rwkv_w0, rwkv_w2, rwkv_a0, rwkv_a2,
                  rwkv_g2, rwkv_k_k, rwkv_k_a, rwkv_r_k, rwkv_ln_w, rwkv_ln_b, ssm_conv_w, ssm_conv_b,
                  ssm_dt_bias, ssm_A_log, ssm_D, ssm_norm_w, fox_f_bias, fox_norm_w, w_out,
                  norm_mlp_pre, norm_mlp_post, w_mlp_up, w_mlp_down):
    row = lambda x: x.reshape(1, -1).astype(F32)
    lo_rows = R_DECAY + R_AAA + R_GATE
    place = lambda m, r0: jnp.pad(m, ((r0, lo_rows - r0 - m.shape[0]), (0, 0)))
    inproj = dict(g=row(norm_mix_pre[l]))
    rwkv = dict(mu=row(rwkv_mu[l]), w0=row(rwkv_w0[l]), w2=place(rwkv_w2[l], 0), a0=row(rwkv_a0[l]),
                a2=place(rwkv_a2[l], R_DECAY), g2=place(rwkv_g2[l], R_DECAY + R_AAA),
                k_k=row(rwkv_k_k[l]), k_a=row(rwkv_k_a[l]), r_k=row(rwkv_r_k[l]),
                ln_w=row(rwkv_ln_w[l]), ln_b=row(rwkv_ln_b[l]))
    ssd = dict(conv_w=ssm_conv_w[l], conv_b=row(ssm_conv_b[l]),
               dt_bias=_pad_cols(row(ssm_dt_bias[l]), LANE),
               a_neg=_pad_cols(row(-jnp.exp(ssm_A_log[l].astype(F32))), LANE),
               d_x=row(jnp.repeat(ssm_D[l], HEAD_DIM)), norm_w=row(ssm_norm_w[l]))
    fox = dict(f_bias=_pad_cols(row(fox_f_bias[l]), LANE), norm_w=row(fox_norm_w[l]))
    wo = w_out[l].astype(BF16)
    mlp = dict(wo_r=wo[:C_R], wo_m=wo[C_R:C_R + C_M], wo_f=wo[C_R + C_M:],
               g_mix_post=row(norm_mix_post[l]), g_mlp_pre=row(norm_mlp_pre[l]),
               g_mlp_post=row(norm_mlp_post[l]), w_up=w_mlp_up[l].astype(BF16),
               w_down=w_mlp_down[l].astype(BF16))
    return inproj, rwkv, ssd, fox, mlp


def kernel(x, norm_mix_pre, norm_mix_post, w_in, rwkv_mu, rwkv_w0, rwkv_w2, rwkv_a0, rwkv_a2, rwkv_g2, rwkv_k_k, rwkv_k_a, rwkv_r_k, rwkv_ln_w, rwkv_ln_b, ssm_conv_w, ssm_conv_b, ssm_dt_bias, ssm_A_log, ssm_D, ssm_norm_w, fox_f_bias, fox_norm_w, w_out, norm_mlp_pre, norm_mlp_post, w_mlp_up, w_mlp_down):
    weights = (norm_mix_pre, norm_mix_post, w_in, rwkv_mu, rwkv_w0, rwkv_w2, rwkv_a0, rwkv_a2, rwkv_g2,
               rwkv_k_k, rwkv_k_a, rwkv_r_k, rwkv_ln_w, rwkv_ln_b, ssm_conv_w, ssm_conv_b, ssm_dt_bias,
               ssm_A_log, ssm_D, ssm_norm_w, fox_f_bias, fox_norm_w, w_out, norm_mlp_pre, norm_mlp_post,
               w_mlp_up, w_mlp_down)
    bsz, t, d = x.shape
    n = bsz * t
    assert t % FOX_BLOCK == 0 and t % SSD_CHUNK == 0 and t % RWKV_CHUNK == 0
    tm = 512 if n % 512 == 0 else 256
    h = x.reshape(n, d)
    w_proj = _inproj_weights(w_in)
    for l in range(w_in.shape[0]):
        inproj, rwkv, ssd, fox, mlp = _layer_params(l, *weights)
        p_r, p_m, p_f = _inproj(h, inproj["g"], w_proj[l], tm)
        y_r = _rwkv(p_r.reshape(bsz, t, -1), rwkv)
        y_m = _ssd(p_m.reshape(bsz, t, -1), ssd)
        p_f3 = p_f.reshape(bsz, t, -1)
        nck = _fox_prep(p_f3, fox["f_bias"])
        y_f = _fox(p_f3, nck, fox["norm_w"])
        h = _mix_mlp(h, y_r.reshape(n, -1), y_m.reshape(n, -1), y_f.reshape(n, -1), mlp, tm)
    return h.reshape(bsz, t, d)
```
